```python
import math
import jax, jax.numpy as jnp
from jax import lax
import numpy as np

D_MODEL = 1024
BATCH = 16
SEQ = 2048
DEPTH = 1

CHUNK = 64
N_META = 16
ATTN_HEADS = 8
ATTN_HEAD_DIM = 64
ATTN_WIDTH = ATTN_HEADS * ATTN_HEAD_DIM
Q_BLOCK = 128
SSM_WIDTH = 512
SSM_GROUP = 16
SSM_GROUPS = SSM_WIDTH // SSM_GROUP
SSM_STATE = 64
D_FF = 2816
CONV_WIDTH = 3
RMS_EPS = 1e-6
SPLITS = (ATTN_WIDTH, 2 * ATTN_WIDTH, 3 * ATTN_WIDTH, 3 * ATTN_WIDTH + ATTN_HEADS,
          3 * ATTN_WIDTH + ATTN_HEADS + SSM_WIDTH)
IN_WIDTH = 3 * ATTN_WIDTH + ATTN_HEADS + SSM_WIDTH + 2 * D_MODEL

kernel_name = "hybrid_fox_s5_convffn_meta"


def _rmsnorm(x, g):
    xf = x.astype(jnp.float32)
    y = xf * lax.rsqrt(jnp.mean(xf * xf, axis=-1, keepdims=True) + RMS_EPS)
    return (y * g.astype(jnp.float32)).astype(x.dtype)


def _forgetting_attention(q, k, v, f_logit, b_forget):
    bsz, seq_len, _ = q.shape
    h, dh = ATTN_HEADS, ATTN_HEAD_DIM
    log_f = jax.nn.log_sigmoid(f_logit.astype(jnp.float32) + b_forget.astype(jnp.float32))
    cum_f = jnp.cumsum(log_f, axis=1)
    lp = -(-seq_len // Q_BLOCK) * Q_BLOCK
    pad = lp - seq_len
    q = jnp.pad(q.reshape(bsz, seq_len, h, dh), ((0, 0), (0, pad), (0, 0), (0, 0)))
    k = jnp.pad(k.reshape(bsz, seq_len, h, dh), ((0, 0), (0, pad), (0, 0), (0, 0)))
    v = jnp.pad(v.reshape(bsz, seq_len, h, dh), ((0, 0), (0, pad), (0, 0), (0, 0)))
    cum_f = jnp.pad(cum_f, ((0, 0), (0, pad), (0, 0)))
    nb = lp // Q_BLOCK
    q_blocks = jnp.moveaxis(q.reshape(bsz, nb, Q_BLOCK, h, dh), 1, 0)
    fq_blocks = jnp.moveaxis(cum_f.reshape(bsz, nb, Q_BLOCK, h), 1, 0)
    q_pos = jnp.arange(lp, dtype=jnp.int32).reshape(nb, Q_BLOCK)
    k_pos = jnp.arange(lp, dtype=jnp.int32)
    fk = jnp.transpose(cum_f, (0, 2, 1))[:, :, None, :]
    scale = 1.0 / math.sqrt(dh)

    def block(args):
        qb, fqb, qp = args
        s = jnp.einsum('bqhd,bkhd->bhqk', qb, k, preferred_element_type=jnp.float32) * scale
        s = s + jnp.transpose(fqb, (0, 2, 1))[..., None] - fk
        mask = k_pos[None, :] <= qp[:, None]
        s = jnp.where(mask[None, None], s, -1e30)
        p = jax.nn.softmax(s, axis=-1)
        return jnp.einsum('bhqk,bkhd->bqhd', p.astype(v.dtype), v)

    out = lax.map(block, (q_blocks, fq_blocks, q_pos))
    out = jnp.moveaxis(out, 0, 1).reshape(bsz, lp, ATTN_WIDTH)
    return out[:, :seq_len]


def _s5(u, lam_re, lam_im, b_re, b_im, c_re, c_im, d_skip, log_dt):
    bsz, seq_len, _ = u.shape
    f32 = jnp.float32
    uf = u.astype(f32)
    ug = uf.reshape(bsz, seq_len, SSM_GROUPS, SSM_GROUP)
    lam = lax.complex(lam_re.astype(f32), lam_im.astype(f32))
    dt = jnp.exp(log_dt.astype(f32))[:, None]
    lam_bar = jnp.exp(lam * dt)
    bmat = lax.complex(b_re.astype(f32), b_im.astype(f32))
    cmat = lax.complex(c_re.astype(f32), c_im.astype(f32))
    b_bar = ((lam_bar - 1.0) / lam)[..., None] * bmat
    bu = jnp.einsum('gpc,blgc->lbgp', b_bar, ug.astype(jnp.complex64))
    a = jnp.broadcast_to(lam_bar, (seq_len, 1, SSM_GROUPS, SSM_STATE))

    def combine(e1, e2):
        a1, b1 = e1
        a2, b2 = e2
        return a1 * a2, a2 * b1 + b2

    _, states = lax.associative_scan(combine, (a, bu), axis=0)
    y = jnp.einsum('gcp,lbgp->blgc', cmat, states).real
    return y.reshape(bsz, seq_len, SSM_WIDTH) + d_skip.astype(f32) * uf


def _mixer(xn, w_in, b_forget, w_attn_out, lam_re, lam_im, b_re, b_im, c_re, c_im,
           d_skip, log_dt, w_glu, w_o):
    proj = xn @ w_in
    q, k, v, f_logit, u, gates = jnp.split(proj, SPLITS, axis=-1)
    y_a = _forgetting_attention(q, k, v, f_logit, b_forget) @ w_attn_out
    z = jax.nn.gelu(_s5(u, lam_re, lam_im, b_re, b_im, c_re, c_im, d_skip, log_dt)).astype(xn.dtype)
    zg = z @ w_glu
    y_s = zg[..., :D_MODEL] * jax.nn.sigmoid(zg[..., D_MODEL:])
    g = jax.nn.sigmoid(gates.astype(jnp.float32))
    merged = g[..., :D_MODEL] * y_a.astype(jnp.float32) + g[..., D_MODEL:] * y_s.astype(jnp.float32)
    return merged.astype(xn.dtype) @ w_o


def _conv_ffn(xn, w_up, conv_w, conv_b, w_down):
    seq_len = xn.shape[1]
    hid = xn @ w_up
    hp = jnp.pad(hid, ((0, 0), (CONV_WIDTH - 1, 0), (0, 0)))
    conv = conv_b + sum(hp[:, j:j + seq_len] * conv_w[j] for j in range(CONV_WIDTH))
    val, gate = jnp.split(conv, 2, axis=-1)
    return (jax.nn.silu(gate) * val) @ w_down


def setup_inputs(seed: int = 0) -> dict:
    key = jax.random.key(seed)
    ks = jax.random.split(key, 24)
    f32 = jnp.float32
    n = lambda k, shape, s: jax.random.normal(k, shape, f32) * s
    L = DEPTH
    log_dt = jax.random.uniform(ks[13], (L, SSM_GROUPS), f32, math.log(1e-3), math.log(1e-1))
    lam_im = jnp.broadcast_to(math.pi * jnp.arange(SSM_STATE, dtype=f32), (L, SSM_GROUPS, SSM_STATE))
    return {
        "x": jax.random.normal(ks[0], (BATCH, SEQ, D_MODEL), f32),
        "meta_tokens": n(ks[1], (N_META, D_MODEL), 1.0),
        "norm_mix_g": 1.0 + n(ks[2], (L, D_MODEL), 0.02),
        "w_in": n(ks[3], (L, D_MODEL, IN_WIDTH), D_MODEL ** -0.5),
        "b_forget": 2.0 + n(ks[4], (L, ATTN_HEADS), 0.5),
        "w_attn_out": n(ks[5], (L, ATTN_WIDTH, D_MODEL), ATTN_WIDTH ** -0.5),
        "ssm_lambda_re": -0.5 + n(ks[6], (L, SSM_GROUPS, SSM_STATE), 0.01),
        "ssm_lambda_im": lam_im + n(ks[7], (L, SSM_GROUPS, SSM_STATE), 0.01),
        "ssm_b_re": n(ks[8], (L, SSM_GROUPS, SSM_STATE, SSM_GROUP), (2 * SSM_GROUP) ** -0.5),
        "ssm_b_im": n(ks[9], (L, SSM_GROUPS, SSM_STATE, SSM_GROUP), (2 * SSM_GROUP) ** -0.5),
        "ssm_c_re": n(ks[10], (L, SSM_GROUPS, SSM_GROUP, SSM_STATE), SSM_STATE ** -0.5),
        "ssm_c_im": n(ks[11], (L, SSM_GROUPS, SSM_GROUP, SSM_STATE), SSM_STATE ** -0.5),
        "ssm_d": n(ks[12], (L, SSM_WIDTH), 1.0),
        "ssm_log_dt": log_dt,
        "w_glu": n(ks[14], (L, SSM_WIDTH, 2 * D_MODEL), SSM_WIDTH ** -0.5),
        "w_o": n(ks[15], (L, D_MODEL, D_MODEL), D_MODEL ** -0.5),
        "norm_ffn_g": 1.0 + n(ks[16], (L, D_MODEL), 0.02),
        "w_ffn_up": n(ks[17], (L, D_MODEL, 2 * D_FF), D_MODEL ** -0.5),
        "ffn_conv_w": n(ks[18], (L, CONV_WIDTH, 2 * D_FF), CONV_WIDTH ** -0.5),
        "ffn_conv_b": n(ks[19], (L, 2 * D_FF), 0.01),
        "w_ffn_down": n(ks[20], (L, D_FF, D_MODEL), D_FF ** -0.5),
        "norm_final_g": 1.0 + n(ks[21], (D_MODEL,), 0.02),
    }


def reference(x, meta_tokens, norm_mix_g, w_in, b_forget, w_attn_out, ssm_lambda_re,
              ssm_lambda_im, ssm_b_re, ssm_b_im, ssm_c_re, ssm_c_im, ssm_d, ssm_log_dt,
              w_glu, w_o, norm_ffn_g, w_ffn_up, ffn_conv_w, ffn_conv_b, w_ffn_down,
              norm_final_g):
    bsz = x.shape[0]
    meta = jnp.broadcast_to(meta_tokens.astype(x.dtype)[None], (bsz, N_META, x.shape[-1]))
    h = jnp.concatenate([meta, x], axis=1)
    for layer in range(DEPTH):
        xn = _rmsnorm(h, norm_mix_g[layer])
        h = h + _mixer(xn, w_in[layer], b_forget[layer], w_attn_out[layer],
                       ssm_lambda_re[layer], ssm_lambda_im[layer], ssm_b_re[layer],
                       ssm_b_im[layer], ssm_c_re[layer], ssm_c_im[layer], ssm_d[layer],
                       ssm_log_dt[layer], w_glu[layer], w_o[layer])
        xn = _rmsnorm(h, norm_ffn_g[layer])
        h = h + _conv_ffn(xn, w_ffn_up[layer], ffn_conv_w[layer], ffn_conv_b[layer],
                          w_ffn_down[layer])
    h = _rmsnorm(h, norm_final_g)
    return h[:, N_META:]
```

```python
import functools
import math

import jax
import jax.numpy as jnp
from jax import lax
from jax.experimental import pallas as pl
from jax.experimental.pallas import tpu as pltpu

F32 = jnp.float32
BF16 = jnp.bfloat16

D_MODEL = 1024
N_META = 16
HEADS = 8
HEAD_DIM = 64
ATTN_W = HEADS * HEAD_DIM
SSM_W = 512
SSM_GROUP = 16
SSM_GROUPS = SSM_W // SSM_GROUP
SSM_STATE = 64
STATE_W = 2 * SSM_GROUPS * SSM_STATE
HALF_GROUPS = SSM_GROUPS // 2
HALF_U = HALF_GROUPS * SSM_GROUP
HALF_STATE = STATE_W // 2
D_FF = 2816
CONV_W = 3
RMS_EPS = 1e-6

LANES = 128
SUBLANES = 8
PAIR_W = 2 * HEAD_DIM
N_PAIRS = HEADS // 2
META_BLK = 128
NEG = -1e30

TM = 512
TQ = 256
S5_CHUNK = 32
S5_META_BATCH = 8
FF_CHUNK = 256
VMEM_LIMIT = 56 * 1024 * 1024


def _const_spec(shape):
    nd = len(shape)
    return pl.BlockSpec(shape, lambda *_: (0,) * nd, pipeline_mode=pl.Buffered(1))


def _params(n_grid):
    return pltpu.CompilerParams(dimension_semantics=("arbitrary",) * n_grid,
                                vmem_limit_bytes=VMEM_LIMIT)


def _rms(x, g):
    ms = jnp.mean(x * x, axis=-1, keepdims=True)
    return x * lax.rsqrt(ms + RMS_EPS) * g


def _dot(a, b):
    return jnp.dot(a, b, preferred_element_type=F32)


def _split3(v):
    a = v.astype(BF16)
    r = v - a.astype(F32)
    b = r.astype(BF16)
    c = (r - b.astype(F32)).astype(BF16)
    return a, b, c


def _inproj_kernel(x_ref, g_ref, wqkv_ref, wu_ref, wg_ref, wfh_ref, wfl_ref, bf_ref, tri_ref,
                   cin_ref, qkv_ref, u_ref, gate_ref, cf_ref, cout_ref, carry_sc, *, nt):
    i = pl.program_id(0)

    @pl.when(i % nt == 0)
    def _():
        carry_sc[...] = cin_ref[...]

    xn = _rms(x_ref[...], g_ref[...])
    xh = xn.astype(BF16)
    xl = (xn - xh.astype(F32)).astype(BF16)

    qkv_ref[...] = _dot(xh, wqkv_ref[...]).astype(BF16)
    u_ref[...] = _dot(xh, wu_ref[...]).astype(BF16)
    gate_ref[...] = jax.nn.sigmoid(_dot(xh, wg_ref[...])).astype(BF16)

    f = _dot(xh, wfh_ref[...]) + _dot(xh, wfl_ref[...]) + _dot(xl, wfh_ref[...]) + bf_ref[...]
    logf = jnp.minimum(f, 0.0) - jnp.log1p(jnp.exp(-jnp.abs(f)))
    a, b, c = _split3(logf)
    tri = tri_ref[...]
    cum = _dot(tri, a) + _dot(tri, b) + _dot(tri, c) + carry_sc[0:1, :]
    cf_ref[...] = cum
    tm = cum.shape[0]
    carry_sc[...] = jnp.broadcast_to(cum[tm - 1:tm, :], carry_sc.shape)
    cout_ref[...] = carry_sc[...]


def _inproj(x2d, carry_in, g1, wqkv, wu, wg, wfh, wfl, bfp, tri, *, bsz, seq, tm):
    nt = seq // tm
    n_tiles = bsz * nt
    rows = bsz * seq
    in_specs = [
        pl.BlockSpec((tm, D_MODEL), lambda i: (i, 0)),
        _const_spec((1, D_MODEL)),
        _const_spec(wqkv.shape),
        _const_spec(wu.shape),
        _const_spec(wg.shape),
        _const_spec(wfh.shape),
        _const_spec(wfl.shape),
        _const_spec((1, LANES)),
        _const_spec((tm, tm)),
        _const_spec((SUBLANES, LANES)),
    ]
    out_specs = [
        pl.BlockSpec((tm, 3 * ATTN_W), lambda i: (i, 0)),
        pl.BlockSpec((tm, SSM_W), lambda i: (i % nt, i // nt)),
        pl.BlockSpec((tm, 2 * D_MODEL), lambda i: (i, 0)),
        pl.BlockSpec((tm, LANES), lambda i: (i, 0)),
        pl.BlockSpec((SUBLANES, LANES), lambda i: (0, 0)),
    ]
    out_shape = [
        jax.ShapeDtypeStruct((rows, 3 * ATTN_W), BF16),
        jax.ShapeDtypeStruct((seq, bsz * SSM_W), BF16),
        jax.ShapeDtypeStruct((rows, 2 * D_MODEL), BF16),
        jax.ShapeDtypeStruct((rows, LANES), F32),
        jax.ShapeDtypeStruct((SUBLANES, LANES), F32),
    ]
    return pl.pallas_call(
        functools.partial(_inproj_kernel, nt=nt),
        grid=(n_tiles,),
        in_specs=in_specs,
        out_specs=out_specs,
        out_shape=out_shape,
        scratch_shapes=[pltpu.VMEM((SUBLANES, LANES), F32)],
        compiler_params=_params(1),
        name="inproj",
    )(x2d, g1, wqkv, wu, wg, wfh, wfl, bfp, tri, carry_in)


def _attn_update(q2, k, v, fq, fk, mask, m_sc, l_sc, acc_sc):
    for hh in range(2):
        s = lax.dot_general(q2[hh], k, (((1,), (1,)), ((), ())), preferred_element_type=F32)
        s = s + fq[hh] - fk[hh]
        if mask is not None:
            s = jnp.where(mask, s, NEG)
        m_old = m_sc[hh]
        m_new = jnp.maximum(m_old, jnp.max(s, axis=1, keepdims=True))
        alpha = jnp.exp(m_old - m_new)
        p = jnp.exp(s - m_new)
        l_sc[hh] = alpha * l_sc[hh] + jnp.sum(p, axis=1, keepdims=True)
        acc_sc[hh] = alpha * acc_sc[hh] + _dot(p.astype(BF16), v)
        m_sc[hh] = m_new


def _pair_queries(q_ref, fq_ref, pair):
    q = q_ref[:, pair * PAIR_W:(pair + 1) * PAIR_W]
    lane = lax.broadcasted_iota(jnp.int32, q.shape, 1)
    zero = jnp.zeros_like(q)
    q2 = (jnp.where(lane < HEAD_DIM, q, zero), jnp.where(lane >= HEAD_DIM, q, zero))
    fq = tuple(fq_ref[:, 2 * pair + hh:2 * pair + hh + 1] for hh in range(2))
    return q2, fq


def _attn_init(m_sc, l_sc, acc_sc):
    m_sc[...] = jnp.full_like(m_sc, NEG)
    l_sc[...] = jnp.zeros_like(l_sc)
    acc_sc[...] = jnp.zeros_like(acc_sc)


def _attn_finish(o_ref, pair, l_sc, acc_sc):
    rows = acc_sc.shape[1]
    lane = lax.broadcasted_iota(jnp.int32, (rows, PAIR_W), 1)
    o0 = acc_sc[0] / l_sc[0]
    o1 = acc_sc[1] / l_sc[1]
    o_ref[:, pair * PAIR_W:(pair + 1) * PAIR_W] = jnp.where(lane < HEAD_DIM, o0, o1).astype(o_ref.dtype)


def _attn_main_kernel(q_ref, kp_ref, vp_ref, k_ref, v_ref, fq_ref, fkp_ref, fk_ref, o_ref,
                      m_sc, l_sc, acc_sc):
    qi = pl.program_id(1)
    tq = q_ref.shape[0]
    col = lax.broadcasted_iota(jnp.int32, (tq, META_BLK), 1)
    meta_mask = col >= META_BLK - N_META
    r = lax.broadcasted_iota(jnp.int32, (tq, tq), 0)
    c = lax.broadcasted_iota(jnp.int32, (tq, tq), 1)
    causal = c <= r
    for pair in range(N_PAIRS):
        lanes = slice(pair * PAIR_W, (pair + 1) * PAIR_W)
        _attn_init(m_sc, l_sc, acc_sc)
        q2, fq = _pair_queries(q_ref, fq_ref, pair)
        fkp = tuple(fkp_ref[2 * pair + hh:2 * pair + hh + 1, :] for hh in range(2))
        _attn_update(q2, kp_ref[:, lanes], vp_ref[:, lanes], fq, fkp, meta_mask, m_sc, l_sc, acc_sc)

        def block(kv, mask):
            off = pl.multiple_of(kv * tq, tq)
            fk = tuple(fk_ref[2 * pair + hh:2 * pair + hh + 1, pl.ds(off, tq)] for hh in range(2))
            _attn_update(q2, k_ref[pl.ds(off, tq), lanes], v_ref[pl.ds(off, tq), lanes], fq, fk,
                         mask, m_sc, l_sc, acc_sc)

        def body(kv, carry):
            block(kv, None)
            return carry

        lax.fori_loop(0, qi, body, 0)
        block(qi, causal)
        _attn_finish(o_ref, pair, l_sc, acc_sc)


def _attn_head_kernel(q_ref, kp_ref, vp_ref, fq_ref, fkp_ref, o_ref, m_sc, l_sc, acc_sc):
    r = lax.broadcasted_iota(jnp.int32, (META_BLK, META_BLK), 0)
    c = lax.broadcasted_iota(jnp.int32, (META_BLK, META_BLK), 1)
    mask = jnp.logical_and(c >= META_BLK - N_META, c <= r)
    for pair in range(N_PAIRS):
        lanes = slice(pair * PAIR_W, (pair + 1) * PAIR_W)
        _attn_init(m_sc, l_sc, acc_sc)
        q2, fq = _pair_queries(q_ref, fq_ref, pair)
        fkp = tuple(fkp_ref[2 * pair + hh:2 * pair + hh + 1, :] for hh in range(2))
        _attn_update(q2, kp_ref[:, lanes], vp_ref[:, lanes], fq, fkp, mask, m_sc, l_sc, acc_sc)
        _attn_finish(o_ref, pair, l_sc, acc_sc)


def _attn_scratch(tq):
    return [pltpu.VMEM((2, tq, 1), F32), pltpu.VMEM((2, tq, 1), F32),
            pltpu.VMEM((2, tq, PAIR_W), F32)]


def _attention(qkv_m, qkv_h, cf_m, cf_h, *, bsz, seq, tm, tq):
    nq = seq // tq
    hm = tm // META_BLK
    kcol, vcol = 1, 2
    cft_m = jnp.transpose(cf_m[:, :HEADS])
    cft_h = jnp.transpose(cf_h[tm - META_BLK:, :HEADS])

    attn_main = pl.pallas_call(
        _attn_main_kernel,
        grid=(bsz, nq),
        in_specs=[
            pl.BlockSpec((tq, ATTN_W), lambda b, qi: (b * nq + qi, 0)),
            pl.BlockSpec((META_BLK, ATTN_W), lambda b, qi: (hm - 1, kcol)),
            pl.BlockSpec((META_BLK, ATTN_W), lambda b, qi: (hm - 1, vcol)),
            pl.BlockSpec((seq, ATTN_W), lambda b, qi: (b, kcol)),
            pl.BlockSpec((seq, ATTN_W), lambda b, qi: (b, vcol)),
            pl.BlockSpec((tq, LANES), lambda b, qi: (b * nq + qi, 0)),
            pl.BlockSpec((SUBLANES, META_BLK), lambda b, qi: (0, 0)),
            pl.BlockSpec((SUBLANES, seq), lambda b, qi: (0, b)),
        ],
        out_specs=pl.BlockSpec((tq, ATTN_W), lambda b, qi: (b * nq + qi, 0)),
        out_shape=jax.ShapeDtypeStruct((bsz * seq, ATTN_W), BF16),
        scratch_shapes=_attn_scratch(tq),
        compiler_params=_params(2),
        name="attn_main",
    )(qkv_m, qkv_h, qkv_h, qkv_m, qkv_m, cf_m, cft_h, cft_m)

    attn_head = pl.pallas_call(
        _attn_head_kernel,
        grid=(1,),
        in_specs=[
            pl.BlockSpec((META_BLK, ATTN_W), lambda i: (hm - 1, 0)),
            pl.BlockSpec((META_BLK, ATTN_W), lambda i: (hm - 1, kcol)),
            pl.BlockSpec((META_BLK, ATTN_W), lambda i: (hm - 1, vcol)),
            pl.BlockSpec((META_BLK, LANES), lambda i: (hm - 1, 0)),
            pl.BlockSpec((SUBLANES, META_BLK), lambda i: (0, 0)),
        ],
        out_specs=pl.BlockSpec((META_BLK, ATTN_W), lambda i: (0, 0)),
        out_shape=jax.ShapeDtypeStruct((META_BLK, ATTN_W), BF16),
        scratch_shapes=_attn_scratch(META_BLK),
        compiler_params=_params(1),
        name="attn_head",
    )(qkv_h, qkv_h, qkv_h, cf_h, cft_h)
    return attn_main, attn_head


def _s5_prep_kernel(lre_ref, lim_ref, ldt_ref, bre_ref, bim_ref, lbr_ref, lbi_ref, bbr_ref, bbi_ref):
    lre = lre_ref[...]
    lim = lim_ref[...]
    dt = jnp.exp(ldt_ref[...])
    mag = jnp.exp(lre * dt)
    ang = lim * dt
    lbr = mag * jnp.cos(ang)
    lbi = mag * jnp.sin(ang)
    lbr_ref[...] = lbr
    lbi_ref[...] = lbi
    nr = lbr - 1.0
    ni = lbi
    den = lre * lre + lim * lim
    cr = (nr * lre + ni * lim) / den
    ci = (ni * lre - nr * lim) / den
    bre = bre_ref[...]
    bim = bim_ref[...]
    bbr_ref[...] = cr * bre - ci * bim
    bbi_ref[...] = cr * bim + ci * bre


def _s5_prep(lam_re, lam_im, log_dt, b_re, b_im):
    n = SSM_GROUPS * SSM_STATE
    col = lambda a: a.reshape(n, 1).astype(F32)
    ldt = jnp.broadcast_to(log_dt.astype(F32)[:, None], (SSM_GROUPS, SSM_STATE))
    return pl.pallas_call(
        _s5_prep_kernel,
        out_shape=[jax.ShapeDtypeStruct((n, 1), F32), jax.ShapeDtypeStruct((n, 1), F32),
                   jax.ShapeDtypeStruct((n, SSM_GROUP), F32), jax.ShapeDtypeStruct((n, SSM_GROUP), F32)],
        name="s5_prep",
    )(col(lam_re), col(lam_im), col(ldt), b_re.reshape(n, SSM_GROUP).astype(F32),
      b_im.reshape(n, SSM_GROUP).astype(F32))


def _gelu_tanh(x):
    c = math.sqrt(2.0 / math.pi)
    return 0.5 * x * (1.0 + jnp.tanh(c * (x + 0.044715 * (x * x * x))))


def _s5_kernel(u_ref, bblk_ref, cblk_ref, lamr_ref, lami_ref, d_ref, init_ref, z_ref, fin_ref,
               utb_sc, st_sc, ztb_sc, carry_sc, *, nb, tc):
    c = pl.program_id(0)
    half_re = HALF_STATE // 2
    lane_chunk = 256

    @pl.when(c == 0)
    def _():
        carry_sc[...] = init_ref[...]

    n_lt = SSM_W // LANES
    for b in range(nb):
        for j in range(n_lt):
            lo = b * SSM_W + j * LANES
            utb_sc[j, pl.ds(b, tc, stride=nb), :] = u_ref[:, lo:lo + LANES].astype(F32)
    uf = jnp.concatenate([utb_sc[j] for j in range(n_lt)], axis=1)
    ub = uf.astype(BF16)
    for h in range(2):
        st_sc[:, h * HALF_STATE:(h + 1) * HALF_STATE] = _dot(ub[:, h * HALF_U:(h + 1) * HALF_U],
                                                            bblk_ref[h])

    def step(prev_ref, prow, rows):
        for h in range(2):
            for j in range(half_re // lane_chunk):
                lam_lo = h * half_re + j * lane_chunk
                lr = lamr_ref[:, lam_lo:lam_lo + lane_chunk]
                li = lami_ref[:, lam_lo:lam_lo + lane_chunk]
                re_lo = h * HALF_STATE + j * lane_chunk
                im_lo = re_lo + half_re
                pre = prev_ref[prow, re_lo:re_lo + lane_chunk]
                pim = prev_ref[prow, im_lo:im_lo + lane_chunk]
                bre = st_sc[rows, re_lo:re_lo + lane_chunk]
                bim = st_sc[rows, im_lo:im_lo + lane_chunk]
                st_sc[rows, re_lo:re_lo + lane_chunk] = lr * pre - li * pim + bre
                st_sc[rows, im_lo:im_lo + lane_chunk] = lr * pim + li * pre + bim

    step(carry_sc, slice(0, nb), slice(0, nb))

    def body(t, carry):
        step(st_sc, pl.ds(pl.multiple_of((t - 1) * nb, nb), nb), pl.ds(pl.multiple_of(t * nb, nb), nb))
        return carry

    lax.fori_loop(1, tc, body, 0)
    carry_sc[...] = st_sc[(tc - 1) * nb:tc * nb, :]
    fin_ref[...] = carry_sc[...]

    ys = []
    for h in range(2):
        sb = st_sc[:, h * HALF_STATE:(h + 1) * HALF_STATE].astype(BF16)
        ys.append(_dot(sb, cblk_ref[h]))
    y = jnp.concatenate(ys, axis=1) + d_ref[...] * uf
    z = _gelu_tanh(y)
    for j in range(n_lt):
        ztb_sc[j] = z[:, j * LANES:(j + 1) * LANES]
    for b in range(nb):
        for j in range(n_lt):
            lo = b * SSM_W + j * LANES
            z_ref[:, lo:lo + LANES] = ztb_sc[j, pl.ds(b, tc, stride=nb), :].astype(z_ref.dtype)


def _s5(u_tm, bblk, cblk, lamr, lami, dskip, init, *, nb, tc):
    steps, width = u_tm.shape
    assert width == nb * SSM_W and steps % tc == 0 and init.shape == (nb, STATE_W)
    return pl.pallas_call(
        functools.partial(_s5_kernel, nb=nb, tc=tc),
        grid=(steps // tc,),
        in_specs=[
            pl.BlockSpec((tc, width), lambda c: (c, 0)),
            _const_spec(bblk.shape),
            _const_spec(cblk.shape),
            _const_spec(lamr.shape),
            _const_spec(lami.shape),
            _const_spec(dskip.shape),
            _const_spec(init.shape),
        ],
        out_specs=[pl.BlockSpec((tc, width), lambda c: (c, 0)),
                   pl.BlockSpec((nb, STATE_W), lambda c: (0, 0))],
        out_shape=[jax.ShapeDtypeStruct((steps, width), BF16),
                   jax.ShapeDtypeStruct((nb, STATE_W), F32)],
        scratch_shapes=[pltpu.VMEM((SSM_W // LANES, tc * nb, LANES), F32),
                        pltpu.VMEM((tc * nb, STATE_W), F32),
                        pltpu.VMEM((SSM_W // LANES, tc * nb, LANES), F32),
                        pltpu.VMEM((nb, STATE_W), F32)],
        compiler_params=_params(1),
        name="s5_scan",
    )(u_tm, bblk, cblk, lamr, lami, dskip, init)


def _s5_block_weights(bbr, bbi, c_re, c_im):
    eye = jnp.eye(HALF_GROUPS, dtype=F32)
    bb = jnp.stack([bbr, bbi]).reshape(2, 2, HALF_GROUPS, SSM_STATE, SSM_GROUP)
    bblk = jnp.einsum('rhgpc,gk->hgcrkp', bb, eye).reshape(2, HALF_U, HALF_STATE)
    cc = jnp.stack([c_re.astype(F32), -c_im.astype(F32)])
    cc = cc.reshape(2, 2, HALF_GROUPS, SSM_GROUP, SSM_STATE)
    cblk = jnp.einsum('rhkcp,kg->hrkpgc', cc, eye).reshape(2, HALF_STATE, HALF_U)
    return bblk.astype(BF16), cblk.astype(BF16)


def _merge_kernel(attn_ref, z_ref, gate_ref, x_ref, wao_ref, wglu_ref, wo_ref, g2_ref,
                  h1_ref, xn2_ref, *, zero_rows):
    ya = _dot(attn_ref[...], wao_ref[...])
    zg = _dot(z_ref[...], wglu_ref[...])
    ys = zg[:, :D_MODEL] * jax.nn.sigmoid(zg[:, D_MODEL:])
    g = gate_ref[...].astype(F32)
    merged = g[:, :D_MODEL] * ya + g[:, D_MODEL:] * ys
    h1 = x_ref[...] + _dot(merged.astype(BF16), wo_ref[...])
    if zero_rows:
        row = lax.broadcasted_iota(jnp.int32, (h1.shape[0], 1), 0)
        h1 = jnp.where(row < zero_rows, 0.0, h1)
    h1_ref[...] = h1
    xn2_ref[...] = _rms(h1, g2_ref[...]).astype(BF16)


def _merge(attn, z_tm, gate, x2d, wao, wglu, wo, g2, *, bsz, seq, tm, zero_rows=0):
    nt = seq // tm
    rows = bsz * seq
    return pl.pallas_call(
        functools.partial(_merge_kernel, zero_rows=zero_rows),
        grid=(bsz * nt,),
        in_specs=[
            pl.BlockSpec((tm, ATTN_W), lambda i: (i, 0)),
            pl.BlockSpec((tm, SSM_W), lambda i: (i % nt, i // nt)),
            pl.BlockSpec((tm, 2 * D_MODEL), lambda i: (i, 0)),
            pl.BlockSpec((tm, D_MODEL), lambda i: (i, 0)),
            _const_spec(wao.shape),
            _const_spec(wglu.shape),
            _const_spec(wo.shape),
            _const_spec((1, D_MODEL)),
        ],
        out_specs=[pl.BlockSpec((tm, D_MODEL), lambda i: (i, 0)),
                   pl.BlockSpec((tm, D_MODEL), lambda i: (i, 0))],
        out_shape=[jax.ShapeDtypeStruct((rows, D_MODEL), F32),
                   jax.ShapeDtypeStruct((rows, D_MODEL), BF16)],
        compiler_params=_params(1),
        name="merge",
    )(attn, z_tm, gate, x2d, wao, wglu, wo, g2)


def _ffn_kernel(xn2_ref, h1_ref, wup_ref, cw_ref, cb_ref, wdn_ref, g3_ref, tin_ref,
                out_ref, tout_ref, carry_sc, *, nt):
    i = pl.program_id(0)

    @pl.when(i % nt == 0)
    def _():
        carry_sc[...] = tin_ref[...]

    xn = xn2_ref[...]
    tm = xn.shape[0]
    row = lax.broadcasted_iota(jnp.int32, (tm, 1), 0)
    acc = jnp.zeros((tm, D_MODEL), F32)
    for c in range(D_FF // FF_CHUNK):
        conv = []
        for part in range(2):
            lo = part * D_FF + c * FF_CHUNK
            cols = slice(lo, lo + FF_CHUNK)
            h = _dot(xn, wup_ref[:, cols])
            prev = carry_sc[:, cols]
            p1 = prev[SUBLANES - 1:SUBLANES, :]
            p2 = prev[SUBLANES - 2:SUBLANES - 1, :]
            hm1 = jnp.where(row == 0, p1, pltpu.roll(h, 1, 0))
            hm2 = jnp.where(row == 0, p2, jnp.where(row == 1, p1, pltpu.roll(h, 2, 0)))
            w = cw_ref[:, cols]
            conv.append(cb_ref[:, cols] + (w[0:1, :] * hm2 + w[1:2, :] * hm1 + w[2:3, :] * h))
            carry_sc[:, cols] = h[tm - SUBLANES:tm, :]
        val, gate = conv
        act = (gate * jax.nn.sigmoid(gate)) * val
        acc = acc + _dot(act.astype(BF16), wdn_ref[c * FF_CHUNK:(c + 1) * FF_CHUNK, :])

    tout_ref[...] = carry_sc[...]
    out_ref[...] = _rms(h1_ref[...] + acc, g3_ref[...])


def _ffn(xn2, h1, wup, cw, cb, wdn, g3, tail_in, *, bsz, seq, tm):
    nt = seq // tm
    return pl.pallas_call(
        functools.partial(_ffn_kernel, nt=nt),
        grid=(bsz * nt,),
        in_specs=[
            pl.BlockSpec((tm, D_MODEL), lambda i: (i, 0)),
            pl.BlockSpec((tm, D_MODEL), lambda i: (i, 0)),
            _const_spec(wup.shape),
            _const_spec(cw.shape),
            _const_spec(cb.shape),
            _const_spec(wdn.shape),
            _const_spec((1, D_MODEL)),
            _const_spec(tail_in.shape),
        ],
        out_specs=[pl.BlockSpec((tm, D_MODEL), lambda i: (i, 0)),
                   pl.BlockSpec(tail_in.shape, lambda i: (0, 0))],
        out_shape=[jax.ShapeDtypeStruct((bsz * seq, D_MODEL), F32),
                   jax.ShapeDtypeStruct(tail_in.shape, F32)],
        scratch_shapes=[pltpu.VMEM(tail_in.shape, F32)],
        compiler_params=_params(1),
        name="convffn",
    )(xn2, h1, wup, cw, cb, wdn, g3, tail_in)


def _forward(x, meta_tokens, norm_mix_g, w_in, b_forget, w_attn_out, lam_re, lam_im, b_re, b_im,
             c_re, c_im, d_skip, log_dt, w_glu, w_o, norm_ffn_g, w_up, conv_w, conv_b, w_down,
             norm_final_g, *, tm, tq, tc):
    bsz, seq, _ = x.shape
    assert seq % tm == 0 and seq % tq == 0 and tm % META_BLK == 0 and seq % tc == 0
    assert META_BLK >= tc >= N_META
    x2d = x.reshape(bsz * seq, D_MODEL)
    head = jnp.concatenate([jnp.zeros((tm - N_META, D_MODEL), F32), meta_tokens.astype(F32)], axis=0)

    q_end, v_end = ATTN_W, 3 * ATTN_W
    f_end = v_end + HEADS
    u_end = f_end + SSM_W
    scale = 1.0 / math.sqrt(HEAD_DIM)
    wqkv = jnp.concatenate([w_in[:, :q_end] * scale, w_in[:, q_end:v_end]], axis=1).astype(BF16)
    wf = jnp.pad(w_in[:, v_end:f_end], ((0, 0), (0, LANES - HEADS)))
    wfh = wf.astype(BF16)
    wfl = (wf - wfh.astype(F32)).astype(BF16)
    bfp = jnp.pad(b_forget.astype(F32), (0, LANES - HEADS)).reshape(1, LANES)
    wu = w_in[:, f_end:u_end].astype(BF16)
    wg = w_in[:, u_end:].astype(BF16)
    tri = (lax.broadcasted_iota(jnp.int32, (tm, tm), 1)
           <= lax.broadcasted_iota(jnp.int32, (tm, tm), 0)).astype(BF16)
    g1 = norm_mix_g.reshape(1, D_MODEL)
    inproj = functools.partial(_inproj, g1=g1, wqkv=wqkv, wu=wu, wg=wg, wfh=wfh, wfl=wfl, bfp=bfp,
                               tri=tri, tm=tm)
    qkv_h, u_h, gate_h, cf_h, f_meta = inproj(head, jnp.zeros((SUBLANES, LANES), F32), bsz=1, seq=tm)
    qkv_m, u_m, gate_m, cf_m, _ = inproj(x2d, f_meta, bsz=bsz, seq=seq)

    attn_m, attn_meta = _attention(qkv_m, qkv_h, cf_m, cf_h, bsz=bsz, seq=seq, tm=tm, tq=tq)
    attn_h = jnp.concatenate([jnp.zeros((tm - META_BLK, ATTN_W), BF16), attn_meta], axis=0)

    lbr, lbi, bbr, bbi = _s5_prep(lam_re, lam_im, log_dt, b_re, b_im)
    bblk, cblk = _s5_block_weights(bbr.reshape(SSM_GROUPS, SSM_STATE, SSM_GROUP),
                                   bbi.reshape(SSM_GROUPS, SSM_STATE, SSM_GROUP), c_re, c_im)
    lamr = lbr.reshape(1, SSM_GROUPS * SSM_STATE)
    lami = lbi.reshape(1, SSM_GROUPS * SSM_STATE)
    dsk = d_skip.astype(F32).reshape(1, SSM_W)
    u_meta = jnp.pad(u_h[tm - tc:], ((0, 0), (0, (S5_META_BATCH - 1) * SSM_W)))
    z_meta, st_meta = _s5(u_meta, bblk, cblk, lamr, lami, dsk,
                          jnp.zeros((S5_META_BATCH, STATE_W), F32), nb=S5_META_BATCH, tc=tc)
    z_h = jnp.concatenate([jnp.zeros((tm - tc, SSM_W), BF16), z_meta[:, :SSM_W]], axis=0)
    z_m, _ = _s5(u_m, bblk, cblk, lamr, lami, dsk, jnp.broadcast_to(st_meta[0:1], (bsz, STATE_W)),
                 nb=bsz, tc=tc)

    merge = functools.partial(_merge, wao=w_attn_out.astype(BF16), wglu=w_glu.astype(BF16),
                              wo=w_o.astype(BF16), g2=norm_ffn_g.reshape(1, D_MODEL), tm=tm)
    h1_h, xn2_h = merge(attn_h, z_h, gate_h, head, bsz=1, seq=tm, zero_rows=tm - N_META)
    h1_m, xn2_m = merge(attn_m, z_m, gate_m, x2d, bsz=bsz, seq=seq)

    cw = jnp.pad(conv_w.astype(F32), ((0, SUBLANES - CONV_W), (0, 0)))
    ffn = functools.partial(_ffn, wup=w_up.astype(BF16), cw=cw, cb=conv_b.astype(F32).reshape(1, 2 * D_FF),
                            wdn=w_down.astype(BF16), g3=norm_final_g.reshape(1, D_MODEL), tm=tm)
    _, tail = ffn(xn2_h, h1_h, tail_in=jnp.zeros((SUBLANES, 2 * D_FF), F32), bsz=1, seq=tm)
    out, _ = ffn(xn2_m, h1_m, tail_in=tail, bsz=bsz, seq=seq)
    return out.reshape(bsz, seq, D_MODEL)


def kernel(x, meta_tokens, norm_mix_g, w_in, b_forget, w_attn_out, ssm_lambda_re, ssm_lambda_im,
           ssm_b_re, ssm_b_im, ssm_c_re, ssm_c_im, ssm_d, ssm_log_dt, w_glu, w_o, norm_ffn_g,
           w_ffn_up, ffn_conv_w, ffn_conv_b, w_ffn_down, norm_final_g):
    assert norm_mix_g.shape[0] == 1, "single-layer block"
    return _forward(x, meta_tokens, norm_mix_g[0], w_in[0], b_forget[0], w_attn_out[0],
                    ssm_lambda_re[0], ssm_lambda_im[0], ssm_b_re[0], ssm_b_im[0], ssm_c_re[0],
                    ssm_c_im[0], ssm_d[0], ssm_log_dt[0], w_glu[0], w_o[0], norm_ffn_g[0],
                    w_ffn_up[0], ffn_conv_w[0], ffn_conv_b[0], w_ffn_down[0], norm_final_g,
                    tm=TM, tq=TQ, tc=S5_CHUNK)
```

```python
import functools
import math

import jax
import jax.numpy as jnp
from jax import lax
from jax.experimental import pallas as pl
from jax.experimental.pallas import tpu as pltpu

F32 = jnp.float32
BF16 = jnp.bfloat16

D_MODEL = 1024
N_META = 16
HEADS = 8
HEAD_DIM = 64
ATTN_W = HEADS * HEAD_DIM
SSM_W = 512
SSM_GROUP = 16
SSM_GROUPS = SSM_W // SSM_GROUP
SSM_STATE = 64
STATE_W = 2 * SSM_GROUPS * SSM_STATE
HALF_GROUPS = SSM_GROUPS // 2
HALF_U = HALF_GROUPS * SSM_GROUP
HALF_STATE = STATE_W // 2
D_FF = 2816
CONV_W = 3
RMS_EPS = 1e-6

LANES = 128
SUBLANES = 8
BF16_ROWS = 16
PAIR_W = 2 * HEAD_DIM
N_PAIRS = HEADS // 2
ACC_ROWS = HEAD_DIM + BF16_ROWS
META_BLK = 128
NEG = -1e30
LOG2E = math.log2(math.e)

TM = 512
TQ = 256
S5_CHUNK = 32
S5_META_BATCH = 8
FF_CHUNK = 256
VMEM_LIMIT = 56 * 1024 * 1024


def _const_spec(shape):
    nd = len(shape)
    return pl.BlockSpec(shape, lambda *_: (0,) * nd, pipeline_mode=pl.Buffered(1))


def _params(n_grid):
    return pltpu.CompilerParams(dimension_semantics=("arbitrary",) * n_grid,
                                vmem_limit_bytes=VMEM_LIMIT)


def _rms(x, g):
    ms = jnp.mean(x * x, axis=-1, keepdims=True)
    return x * lax.rsqrt(ms + RMS_EPS) * g


def _dot(a, b):
    return jnp.dot(a, b, preferred_element_type=F32)


def _split3(v):
    a = v.astype(BF16)
    r = v - a.astype(F32)
    b = r.astype(BF16)
    c = (r - b.astype(F32)).astype(BF16)
    return a, b, c


def _inproj_kernel(x_ref, g_ref, wq_ref, wk_ref, wvt_ref, wu_ref, wg_ref, wfh_ref, wfl_ref, bf_ref,
                   tri_ref, place_ref, ones_ref, cin_ref,
                   q_ref, k_ref, vt_ref, u_ref, gate_ref, cout_ref, carry_sc, *, nt):
    i = pl.program_id(0)

    @pl.when(i % nt == 0)
    def _():
        carry_sc[...] = cin_ref[...]

    xn = _rms(x_ref[...], g_ref[...])
    xh = xn.astype(BF16)
    xl = (xn - xh.astype(F32)).astype(BF16)

    vt_ref[...] = lax.dot_general(wvt_ref[...], xh, (((1,), (1,)), ((), ())),
                                  preferred_element_type=F32).astype(BF16)
    u_ref[...] = _dot(xh, wu_ref[...]).astype(BF16)
    gate_ref[...] = jax.nn.sigmoid(_dot(xh, wg_ref[...])).astype(BF16)

    f = _dot(xh, wfh_ref[...]) + _dot(xh, wfl_ref[...]) + _dot(xl, wfh_ref[...]) + bf_ref[...]
    logf = jnp.minimum(f, 0.0) - jnp.log1p(jnp.exp(-jnp.abs(f)))
    a, b, c = _split3(logf)
    tri = tri_ref[...]
    cum = _dot(tri, a) + _dot(tri, b) + _dot(tri, c) + carry_sc[0:1, :]
    tm = cum.shape[0]
    carry_sc[...] = jnp.broadcast_to(cum[tm - 1:tm, :], carry_sc.shape)
    cout_ref[...] = carry_sc[...]

    fa, fb, fc = _split3(cum * LOG2E)
    aug = _dot(jnp.concatenate([fa, fb, fc], axis=1), place_ref[...]) + ones_ref[...]
    q_ref[...] = (_dot(xh, wq_ref[...]) * (LOG2E / math.sqrt(HEAD_DIM)) + aug[:, :HEADS * LANES]).astype(BF16)
    k_ref[...] = (_dot(xh, wk_ref[...]) + aug[:, HEADS * LANES:]).astype(BF16)


def _inproj(x2d, carry_in, g1, wq, wk, wvt, wu, wg, wfh, wfl, bfp, tri, place, ones, *, bsz, seq, tm):
    nt = seq // tm
    n_tiles = bsz * nt
    rows = bsz * seq
    in_specs = [
        pl.BlockSpec((tm, D_MODEL), lambda i: (i, 0)),
        _const_spec((1, D_MODEL)),
        _const_spec(wq.shape),
        _const_spec(wk.shape),
        _const_spec(wvt.shape),
        _const_spec(wu.shape),
        _const_spec(wg.shape),
        _const_spec(wfh.shape),
        _const_spec(wfl.shape),
        _const_spec((1, LANES)),
        _const_spec((tm, tm)),
        _const_spec(place.shape),
        _const_spec(ones.shape),
        _const_spec((SUBLANES, LANES)),
    ]
    out_specs = [
        pl.BlockSpec((tm, HEADS * LANES), lambda i: (i, 0)),
        pl.BlockSpec((tm, HEADS * LANES), lambda i: (i, 0)),
        pl.BlockSpec((ATTN_W, tm), lambda i: (0, i)),
        pl.BlockSpec((tm, SSM_W), lambda i: (i % nt, i // nt)),
        pl.BlockSpec((tm, 2 * D_MODEL), lambda i: (i, 0)),
        pl.BlockSpec((SUBLANES, LANES), lambda i: (0, 0)),
    ]
    out_shape = [
        jax.ShapeDtypeStruct((rows, HEADS * LANES), BF16),
        jax.ShapeDtypeStruct((rows, HEADS * LANES), BF16),
        jax.ShapeDtypeStruct((ATTN_W, rows), BF16),
        jax.ShapeDtypeStruct((seq, bsz * SSM_W), BF16),
        jax.ShapeDtypeStruct((rows, 2 * D_MODEL), BF16),
        jax.ShapeDtypeStruct((SUBLANES, LANES), F32),
    ]
    return pl.pallas_call(
        functools.partial(_inproj_kernel, nt=nt),
        grid=(n_tiles,),
        in_specs=in_specs,
        out_specs=out_specs,
        out_shape=out_shape,
        scratch_shapes=[pltpu.VMEM((SUBLANES, LANES), F32)],
        compiler_params=_params(1),
        name="inproj",
    )(x2d, g1, wq, wk, wvt, wu, wg, wfh, wfl, bfp, tri, place, ones, carry_in)


def _forget_lane_weights():
    place = jnp.zeros((3 * LANES, 2 * HEADS * LANES), F32)
    ones = jnp.zeros((1, 2 * HEADS * LANES), F32)
    k_off = HEADS * LANES
    for piece in range(3):
        for h in range(HEADS):
            place = place.at[piece * LANES + h, h * LANES + HEAD_DIM + piece].set(1.0)
            place = place.at[piece * LANES + h, k_off + h * LANES + HEAD_DIM + 3 + piece].set(-1.0)
            ones = ones.at[0, h * LANES + HEAD_DIM + 3 + piece].set(1.0)
            ones = ones.at[0, k_off + h * LANES + HEAD_DIM + piece].set(1.0)
    return place.astype(BF16), ones


def _attn_scores(k, q):
    return lax.dot_general(k, q, (((1,), (1,)), ((), ())), preferred_element_type=F32)


def _attn_update(h, blocks, m_sc, acc_sc):
    rows = slice(h * ACC_ROWS, (h + 1) * ACC_ROWS)
    m_old = m_sc[h:h + 1, :]
    m_new = m_old
    for s, _ in blocks:
        m_new = jnp.maximum(m_new, jnp.max(s, axis=0, keepdims=True))
    alpha = jnp.exp2(m_old - m_new)
    pv = None
    for s, vt in blocks:
        p = jnp.exp2(s - m_new).astype(BF16)
        lhs = jnp.concatenate([vt, jnp.ones((BF16_ROWS, vt.shape[1]), BF16)], axis=0)
        d = _dot(lhs, p)
        pv = d if pv is None else pv + d
    acc_sc[rows, :] = alpha * acc_sc[rows, :] + pv
    m_sc[h:h + 1, :] = m_new


def _attn_init(m_sc, acc_sc):
    m_sc[...] = jnp.full_like(m_sc, NEG)
    acc_sc[...] = jnp.zeros_like(acc_sc)


def _attn_finish(o_ref, acc_sc):
    for pair in range(N_PAIRS):
        halves = []
        for h in (2 * pair, 2 * pair + 1):
            base = h * ACC_ROWS
            inv = 1.0 / acc_sc[base + HEAD_DIM:base + HEAD_DIM + 1, :]
            halves.append(acc_sc[base:base + HEAD_DIM, :] * inv)
        ot = jnp.concatenate(halves, axis=0)
        o_ref[:, pair * PAIR_W:(pair + 1) * PAIR_W] = jnp.transpose(ot).astype(o_ref.dtype)


def _attn_main_kernel(q_ref, k_ref, vt_ref, km_ref, vtm_ref, o_ref, m_sc, acc_sc, s0_sc, s1_sc, sm_sc):
    qi = pl.program_id(1)
    tq = q_ref.shape[0]
    _attn_init(m_sc, acc_sc)

    def scores(kv, s_sc):
        off = pl.multiple_of(kv * tq, tq)
        for h in range(HEADS):
            lanes = slice(h * LANES, (h + 1) * LANES)
            s_sc[h] = _attn_scores(k_ref[pl.ds(off, tq), lanes], q_ref[:, lanes])

    def update(kv, s_sc, causal, with_meta):
        off = pl.multiple_of(kv * tq, tq)
        if causal:
            key = lax.broadcasted_iota(jnp.int32, (tq, tq), 0)
            qry = lax.broadcasted_iota(jnp.int32, (tq, tq), 1)
        for h in range(HEADS):
            s = s_sc[h]
            if causal:
                s = jnp.where(key <= qry, s, NEG)
            blocks = [(s, vt_ref[h * HEAD_DIM:(h + 1) * HEAD_DIM, pl.ds(off, tq)])]
            if with_meta:
                blocks.append((sm_sc[h], vtm_ref[h * HEAD_DIM:(h + 1) * HEAD_DIM, :]))
            _attn_update(h, blocks, m_sc, acc_sc)

    for h in range(HEADS):
        lanes = slice(h * LANES, (h + 1) * LANES)
        sm_sc[h] = _attn_scores(km_ref[:, lanes], q_ref[:, lanes])
    scores(0, s0_sc)

    def body(jj, carry):
        j = 2 * jj
        scores(j + 1, s1_sc)
        update(j, s0_sc, False, False)
        scores(j + 2, s0_sc)
        update(j + 1, s1_sc, False, False)
        return carry

    lax.fori_loop(0, qi // 2, body, 0)

    @pl.when(qi % 2 == 0)
    def _():
        update(qi, s0_sc, True, True)

    @pl.when(qi % 2 == 1)
    def _():
        scores(qi, s1_sc)
        update(qi - 1, s0_sc, False, False)
        update(qi, s1_sc, True, True)

    _attn_finish(o_ref, acc_sc)


def _attn_head_kernel(q_ref, km_ref, vtm_ref, o_ref, m_sc, acc_sc):
    tq = q_ref.shape[0]
    _attn_init(m_sc, acc_sc)
    key = lax.broadcasted_iota(jnp.int32, (N_META, tq), 0)
    qry = lax.broadcasted_iota(jnp.int32, (N_META, tq), 1)
    mask = key <= qry - (tq - N_META)
    for h in range(HEADS):
        lanes = slice(h * LANES, (h + 1) * LANES)
        s = jnp.where(mask, _attn_scores(km_ref[:, lanes], q_ref[:, lanes]), NEG)
        _attn_update(h, [(s, vtm_ref[h * HEAD_DIM:(h + 1) * HEAD_DIM, :])], m_sc, acc_sc)
    _attn_finish(o_ref, acc_sc)


def _attn_scratch(tq):
    return [pltpu.VMEM((HEADS, tq), F32), pltpu.VMEM((HEADS * ACC_ROWS, tq), F32)]


def _attention(q_m, k_m, vt_m, q_h, k_h, vt_h, *, bsz, seq, tm, tq):
    nq = seq // tq
    hm = tm // META_BLK
    k_meta = k_h[tm - N_META:]
    vt_meta = vt_h[:, tm - N_META:]

    attn_main = pl.pallas_call(
        _attn_main_kernel,
        grid=(bsz, nq),
        in_specs=[
            pl.BlockSpec((tq, HEADS * LANES), lambda b, qi: (b * nq + qi, 0)),
            pl.BlockSpec((seq, HEADS * LANES), lambda b, qi: (b, 0)),
            pl.BlockSpec((ATTN_W, seq), lambda b, qi: (0, b)),
            pl.BlockSpec(k_meta.shape, lambda b, qi: (0, 0)),
            pl.BlockSpec(vt_meta.shape, lambda b, qi: (0, 0)),
        ],
        out_specs=pl.BlockSpec((tq, ATTN_W), lambda b, qi: (b * nq + qi, 0)),
        out_shape=jax.ShapeDtypeStruct((bsz * seq, ATTN_W), BF16),
        scratch_shapes=_attn_scratch(tq) + [pltpu.VMEM((HEADS, tq, tq), F32),
                                            pltpu.VMEM((HEADS, tq, tq), F32),
                                            pltpu.VMEM((HEADS, N_META, tq), F32)],
        compiler_params=_params(2),
        name="attn_main",
    )(q_m, k_m, vt_m, k_meta, vt_meta)

    attn_head = pl.pallas_call(
        _attn_head_kernel,
        grid=(1,),
        in_specs=[
            pl.BlockSpec((META_BLK, HEADS * LANES), lambda i: (hm - 1, 0)),
            pl.BlockSpec(k_meta.shape, lambda i: (0, 0)),
            pl.BlockSpec(vt_meta.shape, lambda i: (0, 0)),
        ],
        out_specs=pl.BlockSpec((META_BLK, ATTN_W), lambda i: (0, 0)),
        out_shape=jax.ShapeDtypeStruct((META_BLK, ATTN_W), BF16),
        scratch_shapes=_attn_scratch(META_BLK),
        compiler_params=_params(1),
        name="attn_head",
    )(q_h, k_meta, vt_meta)
    return attn_main, attn_head


def _s5_prep_kernel(lre_ref, lim_ref, ldt_ref, bre_ref, bim_ref, lbr_ref, lbi_ref, bbr_ref, bbi_ref):
    lre = lre_ref[...]
    lim = lim_ref[...]
    dt = jnp.exp(ldt_ref[...])
    mag = jnp.exp(lre * dt)
    ang = lim * dt
    lbr = mag * jnp.cos(ang)
    lbi = mag * jnp.sin(ang)
    lbr_ref[...] = lbr
    lbi_ref[...] = lbi
    nr = lbr - 1.0
    ni = lbi
    den = lre * lre + lim * lim
    cr = (nr * lre + ni * lim) / den
    ci = (ni * lre - nr * lim) / den
    bre = bre_ref[...]
    bim = bim_ref[...]
    bbr_ref[...] = cr * bre - ci * bim
    bbi_ref[...] = cr * bim + ci * bre


def _s5_prep(lam_re, lam_im, log_dt, b_re, b_im):
    n = SSM_GROUPS * SSM_STATE
    col = lambda a: a.reshape(n, 1).astype(F32)
    ldt = jnp.broadcast_to(log_dt.astype(F32)[:, None], (SSM_GROUPS, SSM_STATE))
    return pl.pallas_call(
        _s5_prep_kernel,
        out_shape=[jax.ShapeDtypeStruct((n, 1), F32), jax.ShapeDtypeStruct((n, 1), F32),
                   jax.ShapeDtypeStruct((n, SSM_GROUP), F32), jax.ShapeDtypeStruct((n, SSM_GROUP), F32)],
        name="s5_prep",
    )(col(lam_re), col(lam_im), col(ldt), b_re.reshape(n, SSM_GROUP).astype(F32),
      b_im.reshape(n, SSM_GROUP).astype(F32))


def _gelu_tanh(x):
    c = math.sqrt(2.0 / math.pi)
    return 0.5 * x * (1.0 + jnp.tanh(c * (x + 0.044715 * (x * x * x))))


def _s5_kernel(u_ref, bblk_ref, cblk_ref, lamr_ref, lami_ref, d_ref, init_ref, z_ref, fin_ref,
               utb_sc, st_sc, ztb_sc, carry_sc, *, nb, tc):
    c = pl.program_id(0)
    half_re = HALF_STATE // 2
    lane_chunk = 256

    @pl.when(c == 0)
    def _():
        carry_sc[...] = init_ref[...]

    n_lt = SSM_W // LANES
    for b in range(nb):
        for j in range(n_lt):
            lo = b * SSM_W + j * LANES
            utb_sc[j, pl.ds(b, tc, stride=nb), :] = u_ref[:, lo:lo + LANES].astype(F32)
    uf = jnp.concatenate([utb_sc[j] for j in range(n_lt)], axis=1)
    ub = uf.astype(BF16)
    for h in range(2):
        st_sc[:, h * HALF_STATE:(h + 1) * HALF_STATE] = _dot(ub[:, h * HALF_U:(h + 1) * HALF_U],
                                                            bblk_ref[h])

    def step(prev_ref, prow, rows):
        for h in range(2):
            for j in range(half_re // lane_chunk):
                lam_lo = h * half_re + j * lane_chunk
                lr = lamr_ref[:, lam_lo:lam_lo + lane_chunk]
                li = lami_ref[:, lam_lo:lam_lo + lane_chunk]
                re_lo = h * HALF_STATE + j * lane_chunk
                im_lo = re_lo + half_re
                pre = prev_ref[prow, re_lo:re_lo + lane_chunk]
                pim = prev_ref[prow, im_lo:im_lo + lane_chunk]
                bre = st_sc[rows, re_lo:re_lo + lane_chunk]
                bim = st_sc[rows, im_lo:im_lo + lane_chunk]
                st_sc[rows, re_lo:re_lo + lane_chunk] = lr * pre - li * pim + bre
                st_sc[rows, im_lo:im_lo + lane_chunk] = lr * pim + li * pre + bim

    step(carry_sc, slice(0, nb), slice(0, nb))

    def body(t, carry):
        step(st_sc, pl.ds(pl.multiple_of((t - 1) * nb, nb), nb), pl.ds(pl.multiple_of(t * nb, nb), nb))
        return carry

    lax.fori_loop(1, tc, body, 0)
    carry_sc[...] = st_sc[(tc - 1) * nb:tc * nb, :]
    fin_ref[...] = carry_sc[...]

    ys = []
    for h in range(2):
        sb = st_sc[:, h * HALF_STATE:(h + 1) * HALF_STATE].astype(BF16)
        ys.append(_dot(sb, cblk_ref[h]))
    y = jnp.concatenate(ys, axis=1) + d_ref[...] * uf
    z = _gelu_tanh(y)
    for j in range(n_lt):
        ztb_sc[j] = z[:, j * LANES:(j + 1) * LANES]
    for b in range(nb):
        for j in range(n_lt):
            lo = b * SSM_W + j * LANES
            z_ref[:, lo:lo + LANES] = ztb_sc[j, pl.ds(b, tc, stride=nb), :].astype(z_ref.dtype)


def _s5(u_tm, bblk, cblk, lamr, lami, dskip, init, *, nb, tc):
    steps, width = u_tm.shape
    assert width == nb * SSM_W and steps % tc == 0 and init.shape == (nb, STATE_W)
    return pl.pallas_call(
        functools.partial(_s5_kernel, nb=nb, tc=tc),
        grid=(steps // tc,),
        in_specs=[
            pl.BlockSpec((tc, width), lambda c: (c, 0)),
            _const_spec(bblk.shape),
            _const_spec(cblk.shape),
            _const_spec(lamr.shape),
            _const_spec(lami.shape),
            _const_spec(dskip.shape),
            _const_spec(init.shape),
        ],
        out_specs=[pl.BlockSpec((tc, width), lambda c: (c, 0)),
                   pl.BlockSpec((nb, STATE_W), lambda c: (0, 0))],
        out_shape=[jax.ShapeDtypeStruct((steps, width), BF16),
                   jax.ShapeDtypeStruct((nb, STATE_W), F32)],
        scratch_shapes=[pltpu.VMEM((SSM_W // LANES, tc * nb, LANES), F32),
                        pltpu.VMEM((tc * nb, STATE_W), F32),
                        pltpu.VMEM((SSM_W // LANES, tc * nb, LANES), F32),
                        pltpu.VMEM((nb, STATE_W), F32)],
        compiler_params=_params(1),
        name="s5_scan",
    )(u_tm, bblk, cblk, lamr, lami, dskip, init)


def _s5_block_weights(bbr, bbi, c_re, c_im):
    eye = jnp.eye(HALF_GROUPS, dtype=F32)
    bb = jnp.stack([bbr, bbi]).reshape(2, 2, HALF_GROUPS, SSM_STATE, SSM_GROUP)
    bblk = jnp.einsum('rhgpc,gk->hgcrkp', bb, eye).reshape(2, HALF_U, HALF_STATE)
    cc = jnp.stack([c_re.astype(F32), -c_im.astype(F32)])
    cc = cc.reshape(2, 2, HALF_GROUPS, SSM_GROUP, SSM_STATE)
    cblk = jnp.einsum('rhkcp,kg->hrkpgc', cc, eye).reshape(2, HALF_STATE, HALF_U)
    return bblk.astype(BF16), cblk.astype(BF16)


def _merge_kernel(attn_ref, z_ref, gate_ref, x_ref, wao_ref, wglu_ref, wo_ref, g2_ref,
                  h1_ref, xn2_ref, *, zero_rows):
    ya = _dot(attn_ref[...], wao_ref[...])
    zg = _dot(z_ref[...], wglu_ref[...])
    ys = zg[:, :D_MODEL] * jax.nn.sigmoid(zg[:, D_MODEL:])
    g = gate_ref[...].astype(F32)
    merged = g[:, :D_MODEL] * ya + g[:, D_MODEL:] * ys
    h1 = x_ref[...] + _dot(merged.astype(BF16), wo_ref[...])
    if zero_rows:
        row = lax.broadcasted_iota(jnp.int32, (h1.shape[0], 1), 0)
        h1 = jnp.where(row < zero_rows, 0.0, h1)
    h1_ref[...] = h1
    xn2_ref[...] = _rms(h1, g2_ref[...]).astype(BF16)


def _merge(attn, z_tm, gate, x2d, wao, wglu, wo, g2, *, bsz, seq, tm, zero_rows=0):
    nt = seq // tm
    rows = bsz * seq
    return pl.pallas_call(
        functools.partial(_merge_kernel, zero_rows=zero_rows),
        grid=(bsz * nt,),
        in_specs=[
            pl.BlockSpec((tm, ATTN_W), lambda i: (i, 0)),
            pl.BlockSpec((tm, SSM_W), lambda i: (i % nt, i // nt)),
            pl.BlockSpec((tm, 2 * D_MODEL), lambda i: (i, 0)),
            pl.BlockSpec((tm, D_MODEL), lambda i: (i, 0)),
            _const_spec(wao.shape),
            _const_spec(wglu.shape),
            _const_spec(wo.shape),
            _const_spec((1, D_MODEL)),
        ],
        out_specs=[pl.BlockSpec((tm, D_MODEL), lambda i: (i, 0)),
                   pl.BlockSpec((tm, D_MODEL), lambda i: (i, 0))],
        out_shape=[jax.ShapeDtypeStruct((rows, D_MODEL), F32),
                   jax.ShapeDtypeStruct((rows, D_MODEL), BF16)],
        compiler_params=_params(1),
        name="merge",
    )(attn, z_tm, gate, x2d, wao, wglu, wo, g2)


def _ffn_kernel(xn2_ref, h1_ref, wup_ref, cw_ref, cb_ref, wdn_ref, g3_ref, tin_ref,
                out_ref, tout_ref, carry_sc, *, nt):
    i = pl.program_id(0)

    @pl.when(i % nt == 0)
    def _():
        carry_sc[...] = tin_ref[...]

    xn = xn2_ref[...]
    tm = xn.shape[0]
    row = lax.broadcasted_iota(jnp.int32, (tm, 1), 0)
    acc = jnp.zeros((tm, D_MODEL), F32)
    for c in range(D_FF // FF_CHUNK):
        conv = []
        for part in range(2):
            lo = part * D_FF + c * FF_CHUNK
            cols = slice(lo, lo + FF_CHUNK)
            h = _dot(xn, wup_ref[:, cols])
            prev = carry_sc[:, cols]
            p1 = prev[SUBLANES - 1:SUBLANES, :]
            p2 = prev[SUBLANES - 2:SUBLANES - 1, :]
            hm1 = jnp.where(row == 0, p1, pltpu.roll(h, 1, 0))
            hm2 = jnp.where(row == 0, p2, jnp.where(row == 1, p1, pltpu.roll(h, 2, 0)))
            w = cw_ref[:, cols]
            conv.append(cb_ref[:, cols] + (w[0:1, :] * hm2 + w[1:2, :] * hm1 + w[2:3, :] * h))
            carry_sc[:, cols] = h[tm - SUBLANES:tm, :]
        val, gate = conv
        act = (gate * jax.nn.sigmoid(gate)) * val
        acc = acc + _dot(act.astype(BF16), wdn_ref[c * FF_CHUNK:(c + 1) * FF_CHUNK, :])

    tout_ref[...] = carry_sc[...]
    out_ref[...] = _rms(h1_ref[...] + acc, g3_ref[...])


def _ffn(xn2, h1, wup, cw, cb, wdn, g3, tail_in, *, bsz, seq, tm):
    nt = seq // tm
    return pl.pallas_call(
        functools.partial(_ffn_kernel, nt=nt),
        grid=(bsz * nt,),
        in_specs=[
            pl.BlockSpec((tm, D_MODEL), lambda i: (i, 0)),
            pl.BlockSpec((tm, D_MODEL), lambda i: (i, 0)),
            _const_spec(wup.shape),
            _const_spec(cw.shape),
            _const_spec(cb.shape),
            _const_spec(wdn.shape),
            _const_spec((1, D_MODEL)),
            _const_spec(tail_in.shape),
        ],
        out_specs=[pl.BlockSpec((tm, D_MODEL), lambda i: (i, 0)),
                   pl.BlockSpec(tail_in.shape, lambda i: (0, 0))],
        out_shape=[jax.ShapeDtypeStruct((bsz * seq, D_MODEL), F32),
                   jax.ShapeDtypeStruct(tail_in.shape, F32)],
        scratch_shapes=[pltpu.VMEM(tail_in.shape, F32)],
        compiler_params=_params(1),
        name="convffn",
    )(xn2, h1, wup, cw, cb, wdn, g3, tail_in)


def _forward(x, meta_tokens, norm_mix_g, w_in, b_forget, w_attn_out, lam_re, lam_im, b_re, b_im,
             c_re, c_im, d_skip, log_dt, w_glu, w_o, norm_ffn_g, w_up, conv_w, conv_b, w_down,
             norm_final_g, *, tm, tq, tc):
    bsz, seq, _ = x.shape
    assert seq % tm == 0 and seq % tq == 0 and tm % META_BLK == 0 and seq % tc == 0
    assert META_BLK >= tc >= N_META
    x2d = x.reshape(bsz * seq, D_MODEL)
    head = jnp.concatenate([jnp.zeros((tm - N_META, D_MODEL), F32), meta_tokens.astype(F32)], axis=0)

    q_end, k_end, v_end = ATTN_W, 2 * ATTN_W, 3 * ATTN_W
    f_end = v_end + HEADS
    u_end = f_end + SSM_W

    def per_head_lanes(w):
        w = jnp.pad(w.reshape(D_MODEL, HEADS, HEAD_DIM), ((0, 0), (0, 0), (0, LANES - HEAD_DIM)))
        return w.reshape(D_MODEL, HEADS * LANES).astype(BF16)

    wq = per_head_lanes(w_in[:, :q_end])
    wk = per_head_lanes(w_in[:, q_end:k_end])
    wvt = jnp.transpose(w_in[:, k_end:v_end]).astype(BF16)
    place, ones = _forget_lane_weights()
    wf = jnp.pad(w_in[:, v_end:f_end], ((0, 0), (0, LANES - HEADS)))
    wfh = wf.astype(BF16)
    wfl = (wf - wfh.astype(F32)).astype(BF16)
    bfp = jnp.pad(b_forget.astype(F32), (0, LANES - HEADS)).reshape(1, LANES)
    wu = w_in[:, f_end:u_end].astype(BF16)
    wg = w_in[:, u_end:].astype(BF16)
    tri = (lax.broadcasted_iota(jnp.int32, (tm, tm), 1)
           <= lax.broadcasted_iota(jnp.int32, (tm, tm), 0)).astype(BF16)
    g1 = norm_mix_g.reshape(1, D_MODEL)
    inproj = functools.partial(_inproj, g1=g1, wq=wq, wk=wk, wvt=wvt, wu=wu, wg=wg, wfh=wfh, wfl=wfl,
                               bfp=bfp, tri=tri, place=place, ones=ones, tm=tm)
    q_h, k_h, vt_h, u_h, gate_h, f_meta = inproj(head, jnp.zeros((SUBLANES, LANES), F32), bsz=1, seq=tm)
    q_m, k_m, vt_m, u_m, gate_m, _ = inproj(x2d, f_meta, bsz=bsz, seq=seq)

    attn_m, attn_meta = _attention(q_m, k_m, vt_m, q_h, k_h, vt_h, bsz=bsz, seq=seq, tm=tm, tq=tq)
    attn_h = jnp.concatenate([jnp.zeros((tm - META_BLK, ATTN_W), BF16), attn_meta], axis=0)

    lbr, lbi, bbr, bbi = _s5_prep(lam_re, lam_im, log_dt, b_re, b_im)
    bblk, cblk = _s5_block_weights(bbr.reshape(SSM_GROUPS, SSM_STATE, SSM_GROUP),
                                   bbi.reshape(SSM_GROUPS, SSM_STATE, SSM_GROUP), c_re, c_im)
    lamr = lbr.reshape(1, SSM_GROUPS * SSM_STATE)
    lami = lbi.reshape(1, SSM_GROUPS * SSM_STATE)
    dsk = d_skip.astype(F32).reshape(1, SSM_W)
    u_meta = jnp.pad(u_h[tm - tc:], ((0, 0), (0, (S5_META_BATCH - 1) * SSM_W)))
    z_meta, st_meta = _s5(u_meta, bblk, cblk, lamr, lami, dsk,
                          jnp.zeros((S5_META_BATCH, STATE_W), F32), nb=S5_META_BATCH, tc=tc)
    z_h = jnp.concatenate([jnp.zeros((tm - tc, SSM_W), BF16), z_meta[:, :SSM_W]], axis=0)
    z_m, _ = _s5(u_m, bblk, cblk, lamr, lami, dsk, jnp.broadcast_to(st_meta[0:1], (bsz, STATE_W)),
                 nb=bsz, tc=tc)

    merge = functools.partial(_merge, wao=w_attn_out.astype(BF16), wglu=w_glu.astype(BF16),
                              wo=w_o.astype(BF16), g2=norm_ffn_g.reshape(1, D_MODEL), tm=tm)
    h1_h, xn2_h = merge(attn_h, z_h, gate_h, head, bsz=1, seq=tm, zero_rows=tm - N_META)
    h1_m, xn2_m = merge(attn_m, z_m, gate_m, x2d, bsz=bsz, seq=seq)

    cw = jnp.pad(conv_w.astype(F32), ((0, SUBLANES - CONV_W), (0, 0)))
    ffn = functools.partial(_ffn, wup=w_up.astype(BF16), cw=cw, cb=conv_b.astype(F32).reshape(1, 2 * D_FF),
                            wdn=w_down.astype(BF16), g3=norm_final_g.reshape(1, D_MODEL), tm=tm)
    _, tail = ffn(xn2_h, h1_h, tail_in=jnp.zeros((SUBLANES, 2 * D_FF), F32), bsz=1, seq=tm)
    out, _ = ffn(xn2_m, h1_m, tail_in=tail, bsz=bsz, seq=seq)
    return out.reshape(bsz, seq, D_MODEL)


def kernel(x, meta_tokens, norm_mix_g, w_in, b_forget, w_attn_out, ssm_lambda_re, ssm_lambda_im,
           ssm_b_re, ssm_b_im, ssm_c_re, ssm_c_im, ssm_d, ssm_log_dt, w_glu, w_o, norm_ffn_g,
           w_ffn_up, ffn_conv_w, ffn_conv_b, w_ffn_down, norm_final_g):
    assert norm_mix_g.shape[0] == 1, "single-layer block"
    return _forward(x, meta_tokens, norm_mix_g[0], w_in[0], b_forget[0], w_attn_out[0],
                    ssm_lambda_re[0], ssm_lambda_im[0], ssm_b_re[0], ssm_b_im[0], ssm_c_re[0],
                    ssm_c_im[0], ssm_d[0], ssm_log_dt[0], w_glu[0], w_o[0], norm_ffn_g[0],
                    w_ffn_up[0], ffn_conv_w[0], ffn_conv_b[0], w_ffn_down[0], norm_final_g,
                    tm=TM, tq=TQ, tc=S5_CHUNK)
```

```python
import functools
import math

import jax
import jax.numpy as jnp
import numpy as np
from jax import lax
from jax.experimental import pallas as pl
from jax.experimental.pallas import tpu as pltpu

F32 = jnp.float32
BF16 = jnp.bfloat16

D_MODEL = 1024
N_META = 16
HEADS = 8
HEAD_DIM = 64
ATTN_W = HEADS * HEAD_DIM
SSM_W = 512
SSM_GROUP = 16
SSM_GROUPS = SSM_W // SSM_GROUP
SSM_STATE = 64
STATE_W = 2 * SSM_GROUPS * SSM_STATE
HALF_GROUPS = SSM_GROUPS // 2
HALF_U = HALF_GROUPS * SSM_GROUP
HALF_STATE = STATE_W // 2
D_FF = 2816
CONV_W = 3
RMS_EPS = 1e-6

LANES = 128
SUBLANES = 8
BF16_ROWS = 16
PAIR_W = 2 * HEAD_DIM
N_PAIRS = HEADS // 2
ACC_ROWS = HEAD_DIM + BF16_ROWS
META_BLK = 128
NEG = -1e30
LOG2E = math.log2(math.e)

TM = 512
TQ = 256
S5_CHUNK = 32
S5_CHUNKS_PER_STEP = 2
S5_META_BATCH = 8
FF_CHUNK = 256
VMEM_LIMIT = 56 * 1024 * 1024


def _const_spec(shape):
    nd = len(shape)
    return pl.BlockSpec(shape, lambda *_: (0,) * nd, pipeline_mode=pl.Buffered(1))


def _params(n_grid):
    return pltpu.CompilerParams(dimension_semantics=("arbitrary",) * n_grid,
                                vmem_limit_bytes=VMEM_LIMIT)


def _rms(x, g):
    ms = jnp.mean(x * x, axis=-1, keepdims=True)
    return x * lax.rsqrt(ms + RMS_EPS) * g


def _dot(a, b):
    return jnp.dot(a, b, preferred_element_type=F32)


def _pieces3(v):
    a = v.astype(BF16).astype(F32)
    r = v - a
    b = r.astype(BF16).astype(F32)
    c = (r - b).astype(BF16).astype(F32)
    return a, b, c


def _pack_pieces(v):
    a, b, c = _pieces3(v)
    return (a + pltpu.roll(b, HEADS, 1) + pltpu.roll(c, 2 * HEADS, 1)).astype(BF16)


def _unpack_sum(p):
    return p + pltpu.roll(p, LANES - HEADS, 1) + pltpu.roll(p, LANES - 2 * HEADS, 1)


def _inproj_kernel(x_ref, g_ref, wq_ref, wk_ref, wvt_ref, wu_ref, wg_ref, wf_ref, bf_ref,
                   tri_ref, place_ref, ones_ref, cin_ref,
                   q_ref, k_ref, vt_ref, u_ref, gate_ref, cout_ref, carry_sc, *, nt):
    i = pl.program_id(0)

    @pl.when(i % nt == 0)
    def _():
        carry_sc[...] = cin_ref[...]

    xh = _rms(x_ref[...], g_ref[...]).astype(BF16)
    tm = xh.shape[0]

    lane = lax.broadcasted_iota(jnp.int32, (tm, LANES), 1)
    fr = _dot(xh, wf_ref[...])
    f = fr + pltpu.roll(fr, LANES - HEADS, 1) + bf_ref[...]
    logf = jnp.where(lane < HEADS, jnp.minimum(f, 0.0) - jnp.log1p(jnp.exp(-jnp.abs(f))), 0.0)
    cum = _unpack_sum(_dot(tri_ref[...], _pack_pieces(logf))) + carry_sc[0:1, :]
    cum = jnp.where(lane < HEADS, cum, 0.0)
    carry_sc[...] = jnp.broadcast_to(cum[tm - 1:tm, :], carry_sc.shape)
    cout_ref[...] = carry_sc[...]

    vt_ref[...] = lax.dot_general(wvt_ref[...], xh, (((1,), (1,)), ((), ())),
                                  preferred_element_type=F32).astype(BF16)
    u_ref[...] = _dot(xh, wu_ref[...]).astype(BF16)
    gate_ref[...] = jax.nn.sigmoid(_dot(xh, wg_ref[...])).astype(BF16)

    aug = _dot(_pack_pieces(cum * LOG2E), place_ref[...]) + ones_ref[...]
    qr = _dot(xh, wq_ref[...]) * (LOG2E / math.sqrt(HEAD_DIM))
    kr = _dot(xh, wk_ref[...])
    for res, out_ref, aug_lo in ((qr, q_ref, 0), (kr, k_ref, HEADS * LANES)):
        for h in range(HEADS):
            src = res[:, (h // 2) * LANES:(h // 2 + 1) * LANES]
            if h % 2:
                src = pltpu.roll(src, HEAD_DIM, 1)
            blk = jnp.where(lane < HEAD_DIM, src, aug[:, aug_lo + h * LANES:aug_lo + (h + 1) * LANES])
            out_ref[:, h * LANES:(h + 1) * LANES] = blk.astype(BF16)


def _inproj(x2d, carry_in, g1, wq, wk, wvt, wu, wg, wf, bfp, tri, place, ones, *, bsz, seq, tm):
    nt = seq // tm
    n_tiles = bsz * nt
    rows = bsz * seq
    in_specs = [
        pl.BlockSpec((tm, D_MODEL), lambda i: (i, 0)),
        _const_spec((1, D_MODEL)),
        _const_spec(wq.shape),
        _const_spec(wk.shape),
        _const_spec(wvt.shape),
        _const_spec(wu.shape),
        _const_spec(wg.shape),
        _const_spec(wf.shape),
        _const_spec((1, LANES)),
        _const_spec((tm, tm)),
        _const_spec(place.shape),
        _const_spec(ones.shape),
        _const_spec((SUBLANES, LANES)),
    ]
    out_specs = [
        pl.BlockSpec((tm, HEADS * LANES), lambda i: (i, 0)),
        pl.BlockSpec((tm, HEADS * LANES), lambda i: (i, 0)),
        pl.BlockSpec((ATTN_W, tm), lambda i: (0, i)),
        pl.BlockSpec((tm, SSM_W), lambda i: (i % nt, i // nt)),
        pl.BlockSpec((tm, 2 * D_MODEL), lambda i: (i, 0)),
        pl.BlockSpec((SUBLANES, LANES), lambda i: (0, 0)),
    ]
    out_shape = [
        jax.ShapeDtypeStruct((rows, HEADS * LANES), BF16),
        jax.ShapeDtypeStruct((rows, HEADS * LANES), BF16),
        jax.ShapeDtypeStruct((ATTN_W, rows), BF16),
        jax.ShapeDtypeStruct((seq, bsz * SSM_W), BF16),
        jax.ShapeDtypeStruct((rows, 2 * D_MODEL), BF16),
        jax.ShapeDtypeStruct((SUBLANES, LANES), F32),
    ]
    return pl.pallas_call(
        functools.partial(_inproj_kernel, nt=nt),
        grid=(n_tiles,),
        in_specs=in_specs,
        out_specs=out_specs,
        out_shape=out_shape,
        scratch_shapes=[pltpu.VMEM((SUBLANES, LANES), F32)],
        compiler_params=_params(1),
        name="inproj",
    )(x2d, g1, wq, wk, wvt, wu, wg, wf, bfp, tri, place, ones, carry_in)


def _forget_lane_weights():
    place = np.zeros((LANES, 2 * HEADS * LANES), np.float32)
    ones = np.zeros((1, 2 * HEADS * LANES), np.float32)
    k_off = HEADS * LANES
    for piece in range(3):
        for h in range(HEADS):
            place[piece * HEADS + h, h * LANES + HEAD_DIM + piece] = 1.0
            place[piece * HEADS + h, k_off + h * LANES + HEAD_DIM + 3 + piece] = -1.0
            ones[0, h * LANES + HEAD_DIM + 3 + piece] = 1.0
            ones[0, k_off + h * LANES + HEAD_DIM + piece] = 1.0
    return jnp.asarray(place, BF16), jnp.asarray(ones)


def _attn_scores(k, q):
    return lax.dot_general(k, q, (((1,), (1,)), ((), ())), preferred_element_type=F32)


def _attn_update(h, blocks, m_sc, acc_sc):
    rows = slice(h * ACC_ROWS, (h + 1) * ACC_ROWS)
    m_old = m_sc[h:h + 1, :]
    m_new = m_old
    for s, _ in blocks:
        m_new = jnp.maximum(m_new, jnp.max(s, axis=0, keepdims=True))
    alpha = jnp.exp2(m_old - m_new)
    pv = None
    for s, vt in blocks:
        p = jnp.exp2(s - m_new).astype(BF16)
        lhs = jnp.concatenate([vt, jnp.ones((BF16_ROWS, vt.shape[1]), BF16)], axis=0)
        d = _dot(lhs, p)
        pv = d if pv is None else pv + d
    acc_sc[rows, :] = alpha * acc_sc[rows, :] + pv
    m_sc[h:h + 1, :] = m_new


def _attn_init(m_sc, acc_sc):
    m_sc[...] = jnp.full_like(m_sc, NEG)
    acc_sc[...] = jnp.zeros_like(acc_sc)


def _attn_finish(o_ref, acc_sc):
    for pair in range(N_PAIRS):
        halves = []
        for h in (2 * pair, 2 * pair + 1):
            base = h * ACC_ROWS
            inv = 1.0 / acc_sc[base + HEAD_DIM:base + HEAD_DIM + 1, :]
            halves.append(acc_sc[base:base + HEAD_DIM, :] * inv)
        ot = jnp.concatenate(halves, axis=0)
        o_ref[:, pair * PAIR_W:(pair + 1) * PAIR_W] = jnp.transpose(ot).astype(o_ref.dtype)


def _attn_main_kernel(q_ref, k_ref, vt_ref, km_ref, vtm_ref, o_ref, m_sc, acc_sc, s0_sc, s1_sc, sm_sc):
    qi = pl.program_id(1)
    tq = q_ref.shape[0]
    _attn_init(m_sc, acc_sc)

    def scores(kv, s_sc):
        off = pl.multiple_of(kv * tq, tq)
        for h in range(HEADS):
            lanes = slice(h * LANES, (h + 1) * LANES)
            s_sc[h] = _attn_scores(k_ref[pl.ds(off, tq), lanes], q_ref[:, lanes])

    def update(kv, s_sc, causal, with_meta):
        off = pl.multiple_of(kv * tq, tq)
        if causal:
            key = lax.broadcasted_iota(jnp.int32, (tq, tq), 0)
            qry = lax.broadcasted_iota(jnp.int32, (tq, tq), 1)
        for h in range(HEADS):
            s = s_sc[h]
            if causal:
                s = jnp.where(key <= qry, s, NEG)
            blocks = [(s, vt_ref[h * HEAD_DIM:(h + 1) * HEAD_DIM, pl.ds(off, tq)])]
            if with_meta:
                blocks.append((sm_sc[h], vtm_ref[h * HEAD_DIM:(h + 1) * HEAD_DIM, :]))
            _attn_update(h, blocks, m_sc, acc_sc)

    for h in range(HEADS):
        lanes = slice(h * LANES, (h + 1) * LANES)
        sm_sc[h] = _attn_scores(km_ref[:, lanes], q_ref[:, lanes])
    scores(0, s0_sc)

    def body(jj, carry):
        j = 2 * jj
        scores(j + 1, s1_sc)
        update(j, s0_sc, False, False)
        scores(j + 2, s0_sc)
        update(j + 1, s1_sc, False, False)
        return carry

    lax.fori_loop(0, qi // 2, body, 0)

    @pl.when(qi % 2 == 0)
    def _():
        update(qi, s0_sc, True, True)

    @pl.when(qi % 2 == 1)
    def _():
        scores(qi, s1_sc)
        update(qi - 1, s0_sc, False, False)
        update(qi, s1_sc, True, True)

    _attn_finish(o_ref, acc_sc)


def _attn_head_kernel(q_ref, km_ref, vtm_ref, o_ref, m_sc, acc_sc):
    tq = q_ref.shape[0]
    _attn_init(m_sc, acc_sc)
    key = lax.broadcasted_iota(jnp.int32, (N_META, tq), 0)
    qry = lax.broadcasted_iota(jnp.int32, (N_META, tq), 1)
    mask = key <= qry - (tq - N_META)
    for h in range(HEADS):
        lanes = slice(h * LANES, (h + 1) * LANES)
        s = jnp.where(mask, _attn_scores(km_ref[:, lanes], q_ref[:, lanes]), NEG)
        _attn_update(h, [(s, vtm_ref[h * HEAD_DIM:(h + 1) * HEAD_DIM, :])], m_sc, acc_sc)
    _attn_finish(o_ref, acc_sc)


def _attn_scratch(tq):
    return [pltpu.VMEM((HEADS, tq), F32), pltpu.VMEM((HEADS * ACC_ROWS, tq), F32)]


def _attention(q_m, k_m, vt_m, q_h, k_h, vt_h, *, bsz, seq, tm, tq):
    nq = seq // tq
    hm = tm // META_BLK
    k_meta = k_h[tm - N_META:]
    vt_meta = vt_h[:, tm - N_META:]

    attn_main = pl.pallas_call(
        _attn_main_kernel,
        grid=(bsz, nq),
        in_specs=[
            pl.BlockSpec((tq, HEADS * LANES), lambda b, qi: (b * nq + qi, 0)),
            pl.BlockSpec((seq, HEADS * LANES), lambda b, qi: (b, 0)),
            pl.BlockSpec((ATTN_W, seq), lambda b, qi: (0, b)),
            pl.BlockSpec(k_meta.shape, lambda b, qi: (0, 0)),
            pl.BlockSpec(vt_meta.shape, lambda b, qi: (0, 0)),
        ],
        out_specs=pl.BlockSpec((tq, ATTN_W), lambda b, qi: (b * nq + qi, 0)),
        out_shape=jax.ShapeDtypeStruct((bsz * seq, ATTN_W), BF16),
        scratch_shapes=_attn_scratch(tq) + [pltpu.VMEM((HEADS, tq, tq), F32),
                                            pltpu.VMEM((HEADS, tq, tq), F32),
                                            pltpu.VMEM((HEADS, N_META, tq), F32)],
        compiler_params=_params(2),
        name="attn_main",
    )(q_m, k_m, vt_m, k_meta, vt_meta)

    attn_head = pl.pallas_call(
        _attn_head_kernel,
        grid=(1,),
        in_specs=[
            pl.BlockSpec((META_BLK, HEADS * LANES), lambda i: (hm - 1, 0)),
            pl.BlockSpec(k_meta.shape, lambda i: (0, 0)),
            pl.BlockSpec(vt_meta.shape, lambda i: (0, 0)),
        ],
        out_specs=pl.BlockSpec((META_BLK, ATTN_W), lambda i: (0, 0)),
        out_shape=jax.ShapeDtypeStruct((META_BLK, ATTN_W), BF16),
        scratch_shapes=_attn_scratch(META_BLK),
        compiler_params=_params(1),
        name="attn_head",
    )(q_h, k_meta, vt_meta)
    return attn_main, attn_head


def _s5_prep_kernel(lre_ref, lim_ref, ldt_ref, bre_ref, bim_ref, lbr_ref, lbi_ref, bbr_ref, bbi_ref):
    lre = lre_ref[...]
    lim = lim_ref[...]
    dt = jnp.exp(ldt_ref[...])
    mag = jnp.exp(lre * dt)
    ang = lim * dt
    lbr = mag * jnp.cos(ang)
    lbi = mag * jnp.sin(ang)
    lbr_ref[...] = lbr
    lbi_ref[...] = lbi
    nr = lbr - 1.0
    ni = lbi
    den = lre * lre + lim * lim
    cr = (nr * lre + ni * lim) / den
    ci = (ni * lre - nr * lim) / den
    bre = bre_ref[...]
    bim = bim_ref[...]
    bbr_ref[...] = cr * bre - ci * bim
    bbi_ref[...] = cr * bim + ci * bre


def _s5_prep(lam_re, lam_im, log_dt, b_re, b_im):
    n = SSM_GROUPS * SSM_STATE
    col = lambda a: a.reshape(n, 1).astype(F32)
    ldt = jnp.broadcast_to(log_dt.astype(F32)[:, None], (SSM_GROUPS, SSM_STATE))
    return pl.pallas_call(
        _s5_prep_kernel,
        out_shape=[jax.ShapeDtypeStruct((n, 1), F32), jax.ShapeDtypeStruct((n, 1), F32),
                   jax.ShapeDtypeStruct((n, SSM_GROUP), F32), jax.ShapeDtypeStruct((n, SSM_GROUP), F32)],
        name="s5_prep",
    )(col(lam_re), col(lam_im), col(ldt), b_re.reshape(n, SSM_GROUP).astype(F32),
      b_im.reshape(n, SSM_GROUP).astype(F32))


def _gelu_tanh(x):
    c = math.sqrt(2.0 / math.pi)
    return 0.5 * x * (1.0 + jnp.tanh(c * (x + 0.044715 * (x * x * x))))


def _s5_kernel(u_ref, bblk_ref, cblk_ref, lamr_ref, lami_ref, d_ref, init_ref, z_ref, fin_ref,
               utb_sc, st_sc, ztb_sc, carry_sc, *, nb, tc, nch):
    c = pl.program_id(0)
    half_re = HALF_STATE // 2
    lane_chunk = 256
    n_lt = SSM_W // LANES

    @pl.when(c == 0)
    def _():
        carry_sc[...] = init_ref[...]

    for ch in range(nch):
        for b in range(nb):
            for j in range(n_lt):
                lo = b * SSM_W + j * LANES
                utb_sc[ch * n_lt + j, pl.ds(b, tc, stride=nb), :] = (
                    u_ref[ch * tc:(ch + 1) * tc, lo:lo + LANES].astype(F32))
        for h in range(2):
            ub = jnp.concatenate([utb_sc[ch * n_lt + 2 * h], utb_sc[ch * n_lt + 2 * h + 1]],
                                 axis=1).astype(BF16)
            st_sc[ch, :, h * HALF_STATE:(h + 1) * HALF_STATE] = _dot(ub, bblk_ref[h])

    def step(load_prev, ch, rows):
        for h in range(2):
            for j in range(half_re // lane_chunk):
                lam_lo = h * half_re + j * lane_chunk
                lr = lamr_ref[:, lam_lo:lam_lo + lane_chunk]
                li = lami_ref[:, lam_lo:lam_lo + lane_chunk]
                re = slice(h * HALF_STATE + j * lane_chunk, h * HALF_STATE + (j + 1) * lane_chunk)
                im = slice(re.start + half_re, re.stop + half_re)
                pre, pim = load_prev(re), load_prev(im)
                bre, bim = st_sc[ch, rows, re], st_sc[ch, rows, im]
                st_sc[ch, rows, re] = lr * pre - li * pim + bre
                st_sc[ch, rows, im] = lr * pim + li * pre + bim

    last = slice((tc - 1) * nb, tc * nb)
    for ch in range(nch):
        for t in range(tc):
            if t > 0:
                load_prev = lambda cols, ch=ch, t=t: st_sc[ch, (t - 1) * nb:t * nb, cols]
            elif ch > 0:
                load_prev = lambda cols, ch=ch: st_sc[ch - 1, last, cols]
            else:
                load_prev = lambda cols: carry_sc[:, cols]
            step(load_prev, ch, slice(t * nb, (t + 1) * nb))
        ys = []
        for h in range(2):
            sb = st_sc[ch, :, h * HALF_STATE:(h + 1) * HALF_STATE].astype(BF16)
            ys.append(_dot(sb, cblk_ref[h]))
        for j in range(n_lt):
            y = (ys[j // 2][:, (j % 2) * LANES:(j % 2 + 1) * LANES]
                 + d_ref[:, j * LANES:(j + 1) * LANES] * utb_sc[ch * n_lt + j])
            ztb_sc[ch * n_lt + j] = _gelu_tanh(y)
        for b in range(nb):
            for j in range(n_lt):
                lo = b * SSM_W + j * LANES
                z_ref[ch * tc:(ch + 1) * tc, lo:lo + LANES] = (
                    ztb_sc[ch * n_lt + j, pl.ds(b, tc, stride=nb), :].astype(z_ref.dtype))

    carry_sc[...] = st_sc[nch - 1, last, :]
    fin_ref[...] = carry_sc[...]


def _s5(u_tm, bblk, cblk, lamr, lami, dskip, init, *, nb, tc, nch):
    steps, width = u_tm.shape
    blk = tc * nch
    n_lt = SSM_W // LANES
    assert width == nb * SSM_W and steps % blk == 0 and init.shape == (nb, STATE_W)
    return pl.pallas_call(
        functools.partial(_s5_kernel, nb=nb, tc=tc, nch=nch),
        grid=(steps // blk,),
        in_specs=[
            pl.BlockSpec((blk, width), lambda c: (c, 0)),
            _const_spec(bblk.shape),
            _const_spec(cblk.shape),
            _const_spec(lamr.shape),
            _const_spec(lami.shape),
            _const_spec(dskip.shape),
            _const_spec(init.shape),
        ],
        out_specs=[pl.BlockSpec((blk, width), lambda c: (c, 0)),
                   pl.BlockSpec((nb, STATE_W), lambda c: (0, 0))],
        out_shape=[jax.ShapeDtypeStruct((steps, width), BF16),
                   jax.ShapeDtypeStruct((nb, STATE_W), F32)],
        scratch_shapes=[pltpu.VMEM((nch * n_lt, tc * nb, LANES), F32),
                        pltpu.VMEM((nch, tc * nb, STATE_W), F32),
                        pltpu.VMEM((nch * n_lt, tc * nb, LANES), F32),
                        pltpu.VMEM((nb, STATE_W), F32)],
        compiler_params=_params(1),
        name="s5_scan",
    )(u_tm, bblk, cblk, lamr, lami, dskip, init)


def _s5_block_weights(bbr, bbi, c_re, c_im):
    eye = jnp.eye(HALF_GROUPS, dtype=F32)
    bb = jnp.stack([bbr, bbi]).reshape(2, 2, HALF_GROUPS, SSM_STATE, SSM_GROUP)
    bblk = jnp.einsum('rhgpc,gk->hgcrkp', bb, eye).reshape(2, HALF_U, HALF_STATE)
    cc = jnp.stack([c_re.astype(F32), -c_im.astype(F32)])
    cc = cc.reshape(2, 2, HALF_GROUPS, SSM_GROUP, SSM_STATE)
    cblk = jnp.einsum('rhkcp,kg->hrkpgc', cc, eye).reshape(2, HALF_STATE, HALF_U)
    return bblk.astype(BF16), cblk.astype(BF16)


def _merge_kernel(attn_ref, z_ref, gate_ref, x_ref, wao_ref, wglu_ref, wo_ref, g2_ref,
                  h1_ref, xn2_ref, *, zero_rows):
    ya = _dot(attn_ref[...], wao_ref[...])
    zg = _dot(z_ref[...], wglu_ref[...])
    ys = zg[:, :D_MODEL] * jax.nn.sigmoid(zg[:, D_MODEL:])
    g = gate_ref[...].astype(F32)
    merged = g[:, :D_MODEL] * ya + g[:, D_MODEL:] * ys
    h1 = x_ref[...] + _dot(merged.astype(BF16), wo_ref[...])
    if zero_rows:
        row = lax.broadcasted_iota(jnp.int32, (h1.shape[0], 1), 0)
        h1 = jnp.where(row < zero_rows, 0.0, h1)
    h1_ref[...] = h1
    xn2_ref[...] = _rms(h1, g2_ref[...]).astype(BF16)


def _merge(attn, z_tm, gate, x2d, wao, wglu, wo, g2, *, bsz, seq, tm, zero_rows=0):
    nt = seq // tm
    rows = bsz * seq
    return pl.pallas_call(
        functools.partial(_merge_kernel, zero_rows=zero_rows),
        grid=(bsz * nt,),
        in_specs=[
            pl.BlockSpec((tm, ATTN_W), lambda i: (i, 0)),
            pl.BlockSpec((tm, SSM_W), lambda i: (i % nt, i // nt)),
            pl.BlockSpec((tm, 2 * D_MODEL), lambda i: (i, 0)),
            pl.BlockSpec((tm, D_MODEL), lambda i: (i, 0)),
            _const_spec(wao.shape),
            _const_spec(wglu.shape),
            _const_spec(wo.shape),
            _const_spec((1, D_MODEL)),
        ],
        out_specs=[pl.BlockSpec((tm, D_MODEL), lambda i: (i, 0)),
                   pl.BlockSpec((tm, D_MODEL), lambda i: (i, 0))],
        out_shape=[jax.ShapeDtypeStruct((rows, D_MODEL), F32),
                   jax.ShapeDtypeStruct((rows, D_MODEL), BF16)],
        compiler_params=_params(1),
        name="merge",
    )(attn, z_tm, gate, x2d, wao, wglu, wo, g2)


def _ffn_kernel(xn2_ref, h1_ref, wup_ref, cw_ref, cb_ref, wdn_ref, g3_ref, tin_ref,
                out_ref, tout_ref, carry_sc, act_sc, *, nt):
    i = pl.program_id(0)

    @pl.when(i % nt == 0)
    def _():
        carry_sc[...] = tin_ref[...]

    xn = xn2_ref[...]
    tm = xn.shape[0]
    row = lax.broadcasted_iota(jnp.int32, (SUBLANES, 1), 0)
    for c in range(D_FF // FF_CHUNK):
        conv = []
        for part in range(2):
            lo = part * D_FF + c * FF_CHUNK
            cols = slice(lo, lo + FF_CHUNK)
            h = _dot(xn, wup_ref[:, cols])
            prev = carry_sc[:, cols]
            p1 = prev[SUBLANES - 1:SUBLANES, :]
            p2 = prev[SUBLANES - 2:SUBLANES - 1, :]
            r1 = pltpu.roll(h, 1, 0)
            r2 = pltpu.roll(h, 2, 0)
            t1 = jnp.where(row == 0, p1, r1[:SUBLANES])
            t2 = jnp.where(row == 0, p2, jnp.where(row == 1, p1, r2[:SUBLANES]))
            hm1 = jnp.concatenate([t1, r1[SUBLANES:]], axis=0)
            hm2 = jnp.concatenate([t2, r2[SUBLANES:]], axis=0)
            w = cw_ref[:, cols]
            conv.append(cb_ref[:, cols] + (w[0:1, :] * hm2 + w[1:2, :] * hm1 + w[2:3, :] * h))
            carry_sc[:, cols] = h[tm - SUBLANES:tm, :]
        val, gate = conv
        act_sc[:, c * FF_CHUNK:(c + 1) * FF_CHUNK] = ((gate * jax.nn.sigmoid(gate)) * val).astype(BF16)

    tout_ref[...] = carry_sc[...]
    out_ref[...] = _rms(h1_ref[...] + _dot(act_sc[...], wdn_ref[...]), g3_ref[...])


def _ffn(xn2, h1, wup, cw, cb, wdn, g3, tail_in, *, bsz, seq, tm):
    nt = seq // tm
    return pl.pallas_call(
        functools.partial(_ffn_kernel, nt=nt),
        grid=(bsz * nt,),
        in_specs=[
            pl.BlockSpec((tm, D_MODEL), lambda i: (i, 0)),
            pl.BlockSpec((tm, D_MODEL), lambda i: (i, 0)),
            _const_spec(wup.shape),
            _const_spec(cw.shape),
            _const_spec(cb.shape),
            _const_spec(wdn.shape),
            _const_spec((1, D_MODEL)),
            _const_spec(tail_in.shape),
        ],
        out_specs=[pl.BlockSpec((tm, D_MODEL), lambda i: (i, 0)),
                   pl.BlockSpec(tail_in.shape, lambda i: (0, 0))],
        out_shape=[jax.ShapeDtypeStruct((bsz * seq, D_MODEL), F32),
                   jax.ShapeDtypeStruct(tail_in.shape, F32)],
        scratch_shapes=[pltpu.VMEM(tail_in.shape, F32), pltpu.VMEM((tm, D_FF), BF16)],
        compiler_params=_params(1),
        name="convffn",
    )(xn2, h1, wup, cw, cb, wdn, g3, tail_in)


def _forward(x, meta_tokens, norm_mix_g, w_in, b_forget, w_attn_out, lam_re, lam_im, b_re, b_im,
             c_re, c_im, d_skip, log_dt, w_glu, w_o, norm_ffn_g, w_up, conv_w, conv_b, w_down,
             norm_final_g, *, tm, tq, tc):
    bsz, seq, _ = x.shape
    assert seq % tm == 0 and seq % tq == 0 and tm % META_BLK == 0
    assert seq % (tc * S5_CHUNKS_PER_STEP) == 0
    assert META_BLK >= tc >= N_META
    x2d = x.reshape(bsz * seq, D_MODEL)
    head = jnp.concatenate([jnp.zeros((tm - N_META, D_MODEL), F32), meta_tokens.astype(F32)], axis=0)

    q_end, k_end, v_end = ATTN_W, 2 * ATTN_W, 3 * ATTN_W
    f_end = v_end + HEADS
    u_end = f_end + SSM_W
    wq = w_in[:, :q_end].astype(BF16)
    wk = w_in[:, q_end:k_end].astype(BF16)
    wvt = jnp.transpose(w_in[:, k_end:v_end]).astype(BF16)
    place, ones = _forget_lane_weights()
    wf32 = w_in[:, v_end:f_end]
    wf_hi = wf32.astype(BF16)
    wf_lo = (wf32 - wf_hi.astype(F32)).astype(BF16)
    wf = jnp.pad(jnp.concatenate([wf_hi, wf_lo], axis=1), ((0, 0), (0, LANES - 2 * HEADS)))
    bfp = jnp.pad(b_forget.astype(F32), (0, LANES - HEADS)).reshape(1, LANES)
    wu = w_in[:, f_end:u_end].astype(BF16)
    wg = w_in[:, u_end:].astype(BF16)
    tri = jnp.asarray(np.tril(np.ones((tm, tm), np.float32)), BF16)
    g1 = norm_mix_g.reshape(1, D_MODEL)
    inproj = functools.partial(_inproj, g1=g1, wq=wq, wk=wk, wvt=wvt, wu=wu, wg=wg, wf=wf,
                               bfp=bfp, tri=tri, place=place, ones=ones, tm=tm)
    q_h, k_h, vt_h, u_h, gate_h, f_meta = inproj(head, jnp.zeros((SUBLANES, LANES), F32), bsz=1, seq=tm)
    q_m, k_m, vt_m, u_m, gate_m, _ = inproj(x2d, f_meta, bsz=bsz, seq=seq)

    attn_m, attn_meta = _attention(q_m, k_m, vt_m, q_h, k_h, vt_h, bsz=bsz, seq=seq, tm=tm, tq=tq)
    attn_h = jnp.concatenate([jnp.zeros((tm - META_BLK, ATTN_W), BF16), attn_meta], axis=0)

    lbr, lbi, bbr, bbi = _s5_prep(lam_re, lam_im, log_dt, b_re, b_im)
    bblk, cblk = _s5_block_weights(bbr.reshape(SSM_GROUPS, SSM_STATE, SSM_GROUP),
                                   bbi.reshape(SSM_GROUPS, SSM_STATE, SSM_GROUP), c_re, c_im)
    lamr = lbr.reshape(1, SSM_GROUPS * SSM_STATE)
    lami = lbi.reshape(1, SSM_GROUPS * SSM_STATE)
    dsk = d_skip.astype(F32).reshape(1, SSM_W)
    u_meta = jnp.pad(u_h[tm - tc:], ((0, 0), (0, (S5_META_BATCH - 1) * SSM_W)))
    z_meta, st_meta = _s5(u_meta, bblk, cblk, lamr, lami, dsk,
                          jnp.zeros((S5_META_BATCH, STATE_W), F32), nb=S5_META_BATCH, tc=tc, nch=1)
    z_h = jnp.concatenate([jnp.zeros((tm - tc, SSM_W), BF16), z_meta[:, :SSM_W]], axis=0)
    z_m, _ = _s5(u_m, bblk, cblk, lamr, lami, dsk, jnp.broadcast_to(st_meta[0:1], (bsz, STATE_W)),
                 nb=bsz, tc=tc, nch=S5_CHUNKS_PER_STEP)

    merge = functools.partial(_merge, wao=w_attn_out.astype(BF16), wglu=w_glu.astype(BF16),
                              wo=w_o.astype(BF16), g2=norm_ffn_g.reshape(1, D_MODEL), tm=tm)
    h1_h, xn2_h = merge(attn_h, z_h, gate_h, head, bsz=1, seq=tm, zero_rows=tm - N_META)
    h1_m, xn2_m = merge(attn_m, z_m, gate_m, x2d, bsz=bsz, seq=seq)

    cw = jnp.pad(conv_w.astype(F32), ((0, SUBLANES - CONV_W), (0, 0)))
    ffn = functools.partial(_ffn, wup=w_up.astype(BF16), cw=cw, cb=conv_b.astype(F32).reshape(1, 2 * D_FF),
                            wdn=w_down.astype(BF16), g3=norm_final_g.reshape(1, D_MODEL), tm=tm)
    _, tail = ffn(xn2_h, h1_h, tail_in=jnp.zeros((SUBLANES, 2 * D_FF), F32), bsz=1, seq=tm)
    out, _ = ffn(xn2_m, h1_m, tail_in=tail, bsz=bsz, seq=seq)
    return out.reshape(bsz, seq, D_MODEL)


def kernel(x, meta_tokens, norm_mix_g, w_in, b_forget, w_attn_out, ssm_lambda_re, ssm_lambda_im,
           ssm_b_re, ssm_b_im, ssm_c_re, ssm_c_im, ssm_d, ssm_log_dt, w_glu, w_o, norm_ffn_g,
           w_ffn_up, ffn_conv_w, ffn_conv_b, w_ffn_down, norm_final_g):
    assert norm_mix_g.shape[0] == 1, "single-layer block"
    return _forward(x, meta_tokens, norm_mix_g[0], w_in[0], b_forget[0], w_attn_out[0],
                    ssm_lambda_re[0], ssm_lambda_im[0], ssm_b_re[0], ssm_b_im[0], ssm_c_re[0],
                    ssm_c_im[0], ssm_d[0], ssm_log_dt[0], w_glu[0], w_o[0], norm_ffn_g[0],
                    w_ffn_up[0], ffn_conv_w[0], ffn_conv_b[0], w_ffn_down[0], norm_final_g,
                    tm=TM, tq=TQ, tc=S5_CHUNK)
```

```python
import functools
import math

import jax
import jax.numpy as jnp
import numpy as np
from jax import lax
from jax.experimental import pallas as pl
from jax.experimental.pallas import tpu as pltpu

F32 = jnp.float32
BF16 = jnp.bfloat16

D_MODEL = 1024
N_META = 16
HEADS = 8
HEAD_DIM = 64
ATTN_W = HEADS * HEAD_DIM
SSM_W = 512
SSM_GROUP = 16
SSM_GROUPS = SSM_W // SSM_GROUP
SSM_STATE = 64
STATE_W = 2 * SSM_GROUPS * SSM_STATE
HALF_GROUPS = SSM_GROUPS // 2
HALF_U = HALF_GROUPS * SSM_GROUP
HALF_STATE = STATE_W // 2
D_FF = 2816
CONV_W = 3
RMS_EPS = 1e-6

LANES = 128
SUBLANES = 8
BF16_ROWS = 16
PAIR_W = 2 * HEAD_DIM
N_PAIRS = HEADS // 2
ACC_ROWS = HEAD_DIM + BF16_ROWS
META_BLK = 128
NEG = -1e30
LOG2E = math.log2(math.e)

TM = 512
TQ = 256
S5_CHUNK = 32
S5_CHUNKS_PER_STEP = 2
S5_META_BATCH = 8
FF_CHUNK = 256
VMEM_LIMIT = 56 * 1024 * 1024


def _const_spec(shape):
    nd = len(shape)
    return pl.BlockSpec(shape, lambda *_: (0,) * nd, pipeline_mode=pl.Buffered(1))


def _params(n_grid):
    return pltpu.CompilerParams(dimension_semantics=("arbitrary",) * n_grid,
                                vmem_limit_bytes=VMEM_LIMIT)


def _rms(x, g):
    ms = jnp.mean(x * x, axis=-1, keepdims=True)
    return x * lax.rsqrt(ms + RMS_EPS) * g


def _dot(a, b):
    return jnp.dot(a, b, preferred_element_type=F32)


def _pieces3(v):
    a = v.astype(BF16).astype(F32)
    r = v - a
    b = r.astype(BF16).astype(F32)
    c = (r - b).astype(BF16).astype(F32)
    return a, b, c


def _pack_pieces(v):
    a, b, c = _pieces3(v)
    return (a + pltpu.roll(b, HEADS, 1) + pltpu.roll(c, 2 * HEADS, 1)).astype(BF16)


def _unpack_sum(p):
    return p + pltpu.roll(p, LANES - HEADS, 1) + pltpu.roll(p, LANES - 2 * HEADS, 1)


def _inproj_kernel(x_ref, g_ref, wq_ref, wk_ref, wvt_ref, wu_ref, wg_ref, wf_ref, bf_ref,
                   tri_ref, place_ref, ones_ref, cin_ref,
                   q_ref, k_ref, vt_ref, u_ref, gate_ref, cout_ref, carry_sc, *, nt):
    i = pl.program_id(0)

    @pl.when(i % nt == 0)
    def _():
        carry_sc[...] = cin_ref[...]

    xh = _rms(x_ref[...], g_ref[...]).astype(BF16)
    tm = xh.shape[0]

    lane = lax.broadcasted_iota(jnp.int32, (tm, LANES), 1)
    fr = _dot(xh, wf_ref[...])
    f = fr + pltpu.roll(fr, LANES - HEADS, 1) + bf_ref[...]
    logf = jnp.where(lane < HEADS, jnp.minimum(f, 0.0) - jnp.log1p(jnp.exp(-jnp.abs(f))), 0.0)
    cum = _unpack_sum(_dot(tri_ref[...], _pack_pieces(logf))) + carry_sc[0:1, :]
    cum = jnp.where(lane < HEADS, cum, 0.0)
    carry_sc[...] = jnp.broadcast_to(cum[tm - 1:tm, :], carry_sc.shape)
    cout_ref[...] = carry_sc[...]

    vt_ref[...] = lax.dot_general(wvt_ref[...], xh, (((1,), (1,)), ((), ())),
                                  preferred_element_type=F32).astype(BF16)
    u_ref[...] = _dot(xh, wu_ref[...]).astype(BF16)
    gate_ref[...] = jax.nn.sigmoid(_dot(xh, wg_ref[...])).astype(BF16)

    aug = _dot(_pack_pieces(cum * LOG2E), place_ref[...]) + ones_ref[...]
    qr = _dot(xh, wq_ref[...]) * (LOG2E / math.sqrt(HEAD_DIM))
    kr = _dot(xh, wk_ref[...])
    for res, out_ref, aug_lo in ((qr, q_ref, 0), (kr, k_ref, HEADS * LANES)):
        for h in range(HEADS):
            src = res[:, (h // 2) * LANES:(h // 2 + 1) * LANES]
            if h % 2:
                src = pltpu.roll(src, HEAD_DIM, 1)
            blk = jnp.where(lane < HEAD_DIM, src, aug[:, aug_lo + h * LANES:aug_lo + (h + 1) * LANES])
            out_ref[:, h * LANES:(h + 1) * LANES] = blk.astype(BF16)


def _inproj(x2d, carry_in, g1, wq, wk, wvt, wu, wg, wf, bfp, tri, place, ones, *, bsz, seq, tm):
    nt = seq // tm
    n_tiles = bsz * nt
    rows = bsz * seq
    in_specs = [
        pl.BlockSpec((tm, D_MODEL), lambda i: (i, 0)),
        _const_spec((1, D_MODEL)),
        _const_spec(wq.shape),
        _const_spec(wk.shape),
        _const_spec(wvt.shape),
        _const_spec(wu.shape),
        _const_spec(wg.shape),
        _const_spec(wf.shape),
        _const_spec((1, LANES)),
        _const_spec((tm, tm)),
        _const_spec(place.shape),
        _const_spec(ones.shape),
        _const_spec((SUBLANES, LANES)),
    ]
    out_specs = [
        pl.BlockSpec((tm, HEADS * LANES), lambda i: (i, 0)),
        pl.BlockSpec((tm, HEADS * LANES), lambda i: (i, 0)),
        pl.BlockSpec((ATTN_W, tm), lambda i: (0, i)),
        pl.BlockSpec((tm, SSM_W), lambda i: (i % nt, i // nt)),
        pl.BlockSpec((tm, 2 * D_MODEL), lambda i: (i, 0)),
        pl.BlockSpec((SUBLANES, LANES), lambda i: (0, 0)),
    ]
    out_shape = [
        jax.ShapeDtypeStruct((rows, HEADS * LANES), BF16),
        jax.ShapeDtypeStruct((rows, HEADS * LANES), BF16),
        jax.ShapeDtypeStruct((ATTN_W, rows), BF16),
        jax.ShapeDtypeStruct((seq, bsz * SSM_W), BF16),
        jax.ShapeDtypeStruct((rows, 2 * D_MODEL), BF16),
        jax.ShapeDtypeStruct((SUBLANES, LANES), F32),
    ]
    return pl.pallas_call(
        functools.partial(_inproj_kernel, nt=nt),
        grid=(n_tiles,),
        in_specs=in_specs,
        out_specs=out_specs,
        out_shape=out_shape,
        scratch_shapes=[pltpu.VMEM((SUBLANES, LANES), F32)],
        compiler_params=_params(1),
        name="inproj",
    )(x2d, g1, wq, wk, wvt, wu, wg, wf, bfp, tri, place, ones, carry_in)


def _forget_lane_weights():
    place = np.zeros((LANES, 2 * HEADS * LANES), np.float32)
    ones = np.zeros((1, 2 * HEADS * LANES), np.float32)
    k_off = HEADS * LANES
    for piece in range(3):
        for h in range(HEADS):
            place[piece * HEADS + h, h * LANES + HEAD_DIM + piece] = 1.0
            place[piece * HEADS + h, k_off + h * LANES + HEAD_DIM + 3 + piece] = -1.0
            ones[0, h * LANES + HEAD_DIM + 3 + piece] = 1.0
            ones[0, k_off + h * LANES + HEAD_DIM + piece] = 1.0
    return jnp.asarray(place, BF16), jnp.asarray(ones)


def _attn_scores(k, q):
    return lax.dot_general(k, q, (((1,), (1,)), ((), ())), preferred_element_type=F32)


def _attn_update(h, blocks, m_sc, acc_sc, smax=None):
    rows = slice(h * ACC_ROWS, (h + 1) * ACC_ROWS)
    m_old = m_sc[h:h + 1, :]
    if smax is None:
        m_new = m_old
        for s, _ in blocks:
            m_new = jnp.maximum(m_new, jnp.max(s, axis=0, keepdims=True))
    else:
        m_new = jnp.maximum(m_old, smax)
    alpha = jnp.exp2(m_old - m_new)
    pv = None
    for s, vt in blocks:
        p = jnp.exp2(s - m_new).astype(BF16)
        lhs = jnp.concatenate([vt, jnp.ones((BF16_ROWS, vt.shape[1]), BF16)], axis=0)
        d = _dot(lhs, p)
        pv = d if pv is None else pv + d
    acc_sc[rows, :] = alpha * acc_sc[rows, :] + pv
    m_sc[h:h + 1, :] = m_new


def _attn_init(m_sc, acc_sc):
    m_sc[...] = jnp.full_like(m_sc, NEG)
    acc_sc[...] = jnp.zeros_like(acc_sc)


def _attn_finish(o_ref, acc_sc):
    for pair in range(N_PAIRS):
        halves = []
        for h in (2 * pair, 2 * pair + 1):
            base = h * ACC_ROWS
            inv = 1.0 / acc_sc[base + HEAD_DIM:base + HEAD_DIM + 1, :]
            halves.append(acc_sc[base:base + HEAD_DIM, :] * inv)
        ot = jnp.concatenate(halves, axis=0)
        o_ref[:, pair * PAIR_W:(pair + 1) * PAIR_W] = jnp.transpose(ot).astype(o_ref.dtype)


def _attn_main_kernel(q_ref, k_ref, vt_ref, km_ref, vtm_ref, o_ref, m_sc, acc_sc,
                      s0_sc, s1_sc, c0_sc, c1_sc, sm_sc):
    qi = pl.program_id(1)
    tq = q_ref.shape[0]
    _attn_init(m_sc, acc_sc)

    def scores(kv, s_sc, c_sc):
        off = pl.multiple_of(kv * tq, tq)
        for h in range(HEADS):
            lanes = slice(h * LANES, (h + 1) * LANES)
            s = _attn_scores(k_ref[pl.ds(off, tq), lanes], q_ref[:, lanes])
            s_sc[h] = s
            c_sc[h:h + 1, :] = jnp.max(s, axis=0, keepdims=True)

    def update(kv, s_sc, c_sc, diagonal):
        off = pl.multiple_of(kv * tq, tq)
        if diagonal:
            key = lax.broadcasted_iota(jnp.int32, (tq, tq), 0)
            qry = lax.broadcasted_iota(jnp.int32, (tq, tq), 1)
        for h in range(HEADS):
            vt = vt_ref[h * HEAD_DIM:(h + 1) * HEAD_DIM, pl.ds(off, tq)]
            if diagonal:
                s = jnp.where(key <= qry, s_sc[h], NEG)
                _attn_update(h, [(s, vt), (sm_sc[h], vtm_ref[h * HEAD_DIM:(h + 1) * HEAD_DIM, :])],
                             m_sc, acc_sc)
            else:
                _attn_update(h, [(s_sc[h], vt)], m_sc, acc_sc, smax=c_sc[h:h + 1, :])

    for h in range(HEADS):
        lanes = slice(h * LANES, (h + 1) * LANES)
        sm_sc[h] = _attn_scores(km_ref[:, lanes], q_ref[:, lanes])
    scores(0, s0_sc, c0_sc)

    def body(jj, carry):
        j = 2 * jj
        scores(j + 1, s1_sc, c1_sc)
        update(j, s0_sc, c0_sc, False)
        scores(j + 2, s0_sc, c0_sc)
        update(j + 1, s1_sc, c1_sc, False)
        return carry

    lax.fori_loop(0, qi // 2, body, 0)

    @pl.when(qi % 2 == 0)
    def _():
        update(qi, s0_sc, c0_sc, True)

    @pl.when(qi % 2 == 1)
    def _():
        scores(qi, s1_sc, c1_sc)
        update(qi - 1, s0_sc, c0_sc, False)
        update(qi, s1_sc, c1_sc, True)

    _attn_finish(o_ref, acc_sc)


def _attn_head_kernel(q_ref, km_ref, vtm_ref, o_ref, m_sc, acc_sc):
    tq = q_ref.shape[0]
    _attn_init(m_sc, acc_sc)
    key = lax.broadcasted_iota(jnp.int32, (N_META, tq), 0)
    qry = lax.broadcasted_iota(jnp.int32, (N_META, tq), 1)
    mask = key <= qry - (tq - N_META)
    for h in range(HEADS):
        lanes = slice(h * LANES, (h + 1) * LANES)
        s = jnp.where(mask, _attn_scores(km_ref[:, lanes], q_ref[:, lanes]), NEG)
        _attn_update(h, [(s, vtm_ref[h * HEAD_DIM:(h + 1) * HEAD_DIM, :])], m_sc, acc_sc)
    _attn_finish(o_ref, acc_sc)


def _attn_scratch(tq):
    return [pltpu.VMEM((HEADS, tq), F32), pltpu.VMEM((HEADS * ACC_ROWS, tq), F32)]


def _attention(q_m, k_m, vt_m, q_h, k_h, vt_h, *, bsz, seq, tm, tq):
    nq = seq // tq
    hm = tm // META_BLK
    k_meta = k_h[tm - N_META:]
    vt_meta = vt_h[:, tm - N_META:]

    attn_main = pl.pallas_call(
        _attn_main_kernel,
        grid=(bsz, nq),
        in_specs=[
            pl.BlockSpec((tq, HEADS * LANES), lambda b, qi: (b * nq + qi, 0)),
            pl.BlockSpec((seq, HEADS * LANES), lambda b, qi: (b, 0)),
            pl.BlockSpec((ATTN_W, seq), lambda b, qi: (0, b)),
            pl.BlockSpec(k_meta.shape, lambda b, qi: (0, 0)),
            pl.BlockSpec(vt_meta.shape, lambda b, qi: (0, 0)),
        ],
        out_specs=pl.BlockSpec((tq, ATTN_W), lambda b, qi: (b * nq + qi, 0)),
        out_shape=jax.ShapeDtypeStruct((bsz * seq, ATTN_W), BF16),
        scratch_shapes=_attn_scratch(tq) + [pltpu.VMEM((HEADS, tq, tq), F32),
                                            pltpu.VMEM((HEADS, tq, tq), F32),
                                            pltpu.VMEM((HEADS, tq), F32),
                                            pltpu.VMEM((HEADS, tq), F32),
                                            pltpu.VMEM((HEADS, N_META, tq), F32)],
        compiler_params=_params(2),
        name="attn_main",
    )(q_m, k_m, vt_m, k_meta, vt_meta)

    attn_head = pl.pallas_call(
        _attn_head_kernel,
        grid=(1,),
        in_specs=[
            pl.BlockSpec((META_BLK, HEADS * LANES), lambda i: (hm - 1, 0)),
            pl.BlockSpec(k_meta.shape, lambda i: (0, 0)),
            pl.BlockSpec(vt_meta.shape, lambda i: (0, 0)),
        ],
        out_specs=pl.BlockSpec((META_BLK, ATTN_W), lambda i: (0, 0)),
        out_shape=jax.ShapeDtypeStruct((META_BLK, ATTN_W), BF16),
        scratch_shapes=_attn_scratch(META_BLK),
        compiler_params=_params(1),
        name="attn_head",
    )(q_h, k_meta, vt_meta)
    return attn_main, attn_head


def _s5_prep_kernel(lre_ref, lim_ref, ldt_ref, bre_ref, bim_ref, lbr_ref, lbi_ref, bbr_ref, bbi_ref):
    lre = lre_ref[...]
    lim = lim_ref[...]
    dt = jnp.exp(ldt_ref[...])
    mag = jnp.exp(lre * dt)
    ang = lim * dt
    lbr = mag * jnp.cos(ang)
    lbi = mag * jnp.sin(ang)
    lbr_ref[...] = lbr
    lbi_ref[...] = lbi
    nr = lbr - 1.0
    ni = lbi
    den = lre * lre + lim * lim
    cr = (nr * lre + ni * lim) / den
    ci = (ni * lre - nr * lim) / den
    bre = bre_ref[...]
    bim = bim_ref[...]
    bbr_ref[...] = cr * bre - ci * bim
    bbi_ref[...] = cr * bim + ci * bre


def _s5_prep(lam_re, lam_im, log_dt, b_re, b_im):
    n = SSM_GROUPS * SSM_STATE
    col = lambda a: a.reshape(n, 1).astype(F32)
    ldt = jnp.broadcast_to(log_dt.astype(F32)[:, None], (SSM_GROUPS, SSM_STATE))
    return pl.pallas_call(
        _s5_prep_kernel,
        out_shape=[jax.ShapeDtypeStruct((n, 1), F32), jax.ShapeDtypeStruct((n, 1), F32),
                   jax.ShapeDtypeStruct((n, SSM_GROUP), F32), jax.ShapeDtypeStruct((n, SSM_GROUP), F32)],
        name="s5_prep",
    )(col(lam_re), col(lam_im), col(ldt), b_re.reshape(n, SSM_GROUP).astype(F32),
      b_im.reshape(n, SSM_GROUP).astype(F32))


def _gelu_tanh(x):
    c = math.sqrt(2.0 / math.pi)
    return 0.5 * x * (1.0 + jnp.tanh(c * (x + 0.044715 * (x * x * x))))


def _s5_kernel(u_ref, bblk_ref, cblk_ref, lamr_ref, lami_ref, d_ref, init_ref, z_ref, fin_ref,
               utb_sc, st_sc, ztb_sc, carry_sc, *, nb, tc, nch):
    c = pl.program_id(0)
    half_re = HALF_STATE // 2
    lane_chunk = 256
    n_lt = SSM_W // LANES

    @pl.when(c == 0)
    def _():
        carry_sc[...] = init_ref[...]

    for ch in range(nch):
        for b in range(nb):
            for j in range(n_lt):
                lo = b * SSM_W + j * LANES
                utb_sc[ch * n_lt + j, pl.ds(b, tc, stride=nb), :] = (
                    u_ref[ch * tc:(ch + 1) * tc, lo:lo + LANES].astype(F32))
        for h in range(2):
            ub = jnp.concatenate([utb_sc[ch * n_lt + 2 * h], utb_sc[ch * n_lt + 2 * h + 1]],
                                 axis=1).astype(BF16)
            st_sc[ch, :, h * HALF_STATE:(h + 1) * HALF_STATE] = _dot(ub, bblk_ref[h])

    def step(load_prev, ch, rows):
        for h in range(2):
            for j in range(half_re // lane_chunk):
                lam_lo = h * half_re + j * lane_chunk
                lr = lamr_ref[:, lam_lo:lam_lo + lane_chunk]
                li = lami_ref[:, lam_lo:lam_lo + lane_chunk]
                re = slice(h * HALF_STATE + j * lane_chunk, h * HALF_STATE + (j + 1) * lane_chunk)
                im = slice(re.start + half_re, re.stop + half_re)
                pre, pim = load_prev(re), load_prev(im)
                bre, bim = st_sc[ch, rows, re], st_sc[ch, rows, im]
                st_sc[ch, rows, re] = lr * pre - li * pim + bre
                st_sc[ch, rows, im] = lr * pim + li * pre + bim

    last = slice((tc - 1) * nb, tc * nb)
    for ch in range(nch):
        for t in range(tc):
            if t > 0:
                load_prev = lambda cols, ch=ch, t=t: st_sc[ch, (t - 1) * nb:t * nb, cols]
            elif ch > 0:
                load_prev = lambda cols, ch=ch: st_sc[ch - 1, last, cols]
            else:
                load_prev = lambda cols: carry_sc[:, cols]
            step(load_prev, ch, slice(t * nb, (t + 1) * nb))
        ys = []
        for h in range(2):
            sb = st_sc[ch, :, h * HALF_STATE:(h + 1) * HALF_STATE].astype(BF16)
            ys.append(_dot(sb, cblk_ref[h]))
        for j in range(n_lt):
            y = (ys[j // 2][:, (j % 2) * LANES:(j % 2 + 1) * LANES]
                 + d_ref[:, j * LANES:(j + 1) * LANES] * utb_sc[ch * n_lt + j])
            ztb_sc[ch * n_lt + j] = _gelu_tanh(y)
        for b in range(nb):
            for j in range(n_lt):
                lo = b * SSM_W + j * LANES
                z_ref[ch * tc:(ch + 1) * tc, lo:lo + LANES] = (
                    ztb_sc[ch * n_lt + j, pl.ds(b, tc, stride=nb), :].astype(z_ref.dtype))

    carry_sc[...] = st_sc[nch - 1, last, :]
    fin_ref[...] = carry_sc[...]


def _s5(u_tm, bblk, cblk, lamr, lami, dskip, init, *, nb, tc, nch):
    steps, width = u_tm.shape
    blk = tc * nch
    n_lt = SSM_W // LANES
    assert width == nb * SSM_W and steps % blk == 0 and init.shape == (nb, STATE_W)
    return pl.pallas_call(
        functools.partial(_s5_kernel, nb=nb, tc=tc, nch=nch),
        grid=(steps // blk,),
        in_specs=[
            pl.BlockSpec((blk, width), lambda c: (c, 0)),
            _const_spec(bblk.shape),
            _const_spec(cblk.shape),
            _const_spec(lamr.shape),
            _const_spec(lami.shape),
            _const_spec(dskip.shape),
            _const_spec(init.shape),
        ],
        out_specs=[pl.BlockSpec((blk, width), lambda c: (c, 0)),
                   pl.BlockSpec((nb, STATE_W), lambda c: (0, 0))],
        out_shape=[jax.ShapeDtypeStruct((steps, width), BF16),
                   jax.ShapeDtypeStruct((nb, STATE_W), F32)],
        scratch_shapes=[pltpu.VMEM((nch * n_lt, tc * nb, LANES), F32),
                        pltpu.VMEM((nch, tc * nb, STATE_W), F32),
                        pltpu.VMEM((nch * n_lt, tc * nb, LANES), F32),
                        pltpu.VMEM((nb, STATE_W), F32)],
        compiler_params=_params(1),
        name="s5_scan",
    )(u_tm, bblk, cblk, lamr, lami, dskip, init)


def _s5_block_weights(bbr, bbi, c_re, c_im):
    eye = jnp.eye(HALF_GROUPS, dtype=F32)
    bb = jnp.stack([bbr, bbi]).reshape(2, 2, HALF_GROUPS, SSM_STATE, SSM_GROUP)
    bblk = jnp.einsum('rhgpc,gk->hgcrkp', bb, eye).reshape(2, HALF_U, HALF_STATE)
    cc = jnp.stack([c_re.astype(F32), -c_im.astype(F32)])
    cc = cc.reshape(2, 2, HALF_GROUPS, SSM_GROUP, SSM_STATE)
    cblk = jnp.einsum('rhkcp,kg->hrkpgc', cc, eye).reshape(2, HALF_STATE, HALF_U)
    return bblk.astype(BF16), cblk.astype(BF16)


def _mixffn_kernel(attn_ref, z_ref, gate_ref, x_ref, wao_ref, wglu_ref, wo_ref, g2_ref,
                   wup_ref, cw_ref, cb_ref, wdn_ref, g3_ref, tin_ref,
                   out_ref, tout_ref, carry_sc, act_sc, h1_sc, *, nt, zero_rows):
    i = pl.program_id(0)

    @pl.when(i % nt == 0)
    def _():
        carry_sc[...] = tin_ref[...]

    ya = _dot(attn_ref[...], wao_ref[...])
    zg = _dot(z_ref[...], wglu_ref[...])
    ys = zg[:, :D_MODEL] * jax.nn.sigmoid(zg[:, D_MODEL:])
    g = gate_ref[...].astype(F32)
    merged = g[:, :D_MODEL] * ya + g[:, D_MODEL:] * ys
    h1 = x_ref[...] + _dot(merged.astype(BF16), wo_ref[...])
    tm = h1.shape[0]
    if zero_rows:
        r = lax.broadcasted_iota(jnp.int32, (tm, 1), 0)
        h1 = jnp.where(r < zero_rows, 0.0, h1)
    h1_sc[...] = h1
    xn = _rms(h1, g2_ref[...]).astype(BF16)

    row = lax.broadcasted_iota(jnp.int32, (SUBLANES, 1), 0)
    for c in range(D_FF // FF_CHUNK):
        conv = []
        for part in range(2):
            lo = part * D_FF + c * FF_CHUNK
            cols = slice(lo, lo + FF_CHUNK)
            h = _dot(xn, wup_ref[:, cols])
            prev = carry_sc[:, cols]
            p1 = prev[SUBLANES - 1:SUBLANES, :]
            p2 = prev[SUBLANES - 2:SUBLANES - 1, :]
            r1 = pltpu.roll(h, 1, 0)
            r2 = pltpu.roll(h, 2, 0)
            t1 = jnp.where(row == 0, p1, r1[:SUBLANES])
            t2 = jnp.where(row == 0, p2, jnp.where(row == 1, p1, r2[:SUBLANES]))
            hm1 = jnp.concatenate([t1, r1[SUBLANES:]], axis=0)
            hm2 = jnp.concatenate([t2, r2[SUBLANES:]], axis=0)
            w = cw_ref[:, cols]
            conv.append(cb_ref[:, cols] + (w[0:1, :] * hm2 + w[1:2, :] * hm1 + w[2:3, :] * h))
            carry_sc[:, cols] = h[tm - SUBLANES:tm, :]
        val, gate = conv
        act_sc[:, c * FF_CHUNK:(c + 1) * FF_CHUNK] = ((gate * jax.nn.sigmoid(gate)) * val).astype(BF16)

    tout_ref[...] = carry_sc[...]
    out_ref[...] = _rms(h1_sc[...] + _dot(act_sc[...], wdn_ref[...]), g3_ref[...])


def _mixffn(attn, z_tm, gate, x2d, tail_in, wao, wglu, wo, g2, wup, cw, cb, wdn, g3, *,
            bsz, seq, tm, zero_rows=0):
    nt = seq // tm
    return pl.pallas_call(
        functools.partial(_mixffn_kernel, nt=nt, zero_rows=zero_rows),
        grid=(bsz * nt,),
        in_specs=[
            pl.BlockSpec((tm, ATTN_W), lambda i: (i, 0)),
            pl.BlockSpec((tm, SSM_W), lambda i: (i % nt, i // nt)),
            pl.BlockSpec((tm, 2 * D_MODEL), lambda i: (i, 0)),
            pl.BlockSpec((tm, D_MODEL), lambda i: (i, 0)),
            _const_spec(wao.shape),
            _const_spec(wglu.shape),
            _const_spec(wo.shape),
            _const_spec((1, D_MODEL)),
            _const_spec(wup.shape),
            _const_spec(cw.shape),
            _const_spec(cb.shape),
            _const_spec(wdn.shape),
            _const_spec((1, D_MODEL)),
            _const_spec(tail_in.shape),
        ],
        out_specs=[pl.BlockSpec((tm, D_MODEL), lambda i: (i, 0)),
                   pl.BlockSpec(tail_in.shape, lambda i: (0, 0))],
        out_shape=[jax.ShapeDtypeStruct((bsz * seq, D_MODEL), F32),
                   jax.ShapeDtypeStruct(tail_in.shape, F32)],
        scratch_shapes=[pltpu.VMEM(tail_in.shape, F32), pltpu.VMEM((tm, D_FF), BF16),
                        pltpu.VMEM((tm, D_MODEL), F32)],
        compiler_params=_params(1),
        name="mixffn",
    )(attn, z_tm, gate, x2d, wao, wglu, wo, g2, wup, cw, cb, wdn, g3, tail_in)


def _forward(x, meta_tokens, norm_mix_g, w_in, b_forget, w_attn_out, lam_re, lam_im, b_re, b_im,
             c_re, c_im, d_skip, log_dt, w_glu, w_o, norm_ffn_g, w_up, conv_w, conv_b, w_down,
             norm_final_g, *, tm, tq, tc):
    bsz, seq, _ = x.shape
    assert seq % tm == 0 and seq % tq == 0
    assert seq % (tc * S5_CHUNKS_PER_STEP) == 0
    th = META_BLK
    assert th >= tc >= N_META
    x2d = x.reshape(bsz * seq, D_MODEL)
    head = jnp.concatenate([jnp.zeros((th - N_META, D_MODEL), F32), meta_tokens.astype(F32)], axis=0)

    q_end, k_end, v_end = ATTN_W, 2 * ATTN_W, 3 * ATTN_W
    f_end = v_end + HEADS
    u_end = f_end + SSM_W
    wq = w_in[:, :q_end].astype(BF16)
    wk = w_in[:, q_end:k_end].astype(BF16)
    wvt = jnp.transpose(w_in[:, k_end:v_end]).astype(BF16)
    place, ones = _forget_lane_weights()
    wf32 = w_in[:, v_end:f_end]
    wf_hi = wf32.astype(BF16)
    wf_lo = (wf32 - wf_hi.astype(F32)).astype(BF16)
    wf = jnp.pad(jnp.concatenate([wf_hi, wf_lo], axis=1), ((0, 0), (0, LANES - 2 * HEADS)))
    bfp = jnp.pad(b_forget.astype(F32), (0, LANES - HEADS)).reshape(1, LANES)
    wu = w_in[:, f_end:u_end].astype(BF16)
    wg = w_in[:, u_end:].astype(BF16)
    tri = lambda n: jnp.asarray(np.tril(np.ones((n, n), np.float32)), BF16)
    g1 = norm_mix_g.reshape(1, D_MODEL)
    inproj = functools.partial(_inproj, g1=g1, wq=wq, wk=wk, wvt=wvt, wu=wu, wg=wg, wf=wf,
                               bfp=bfp, place=place, ones=ones)
    q_h, k_h, vt_h, u_h, gate_h, f_meta = inproj(head, jnp.zeros((SUBLANES, LANES), F32), tri=tri(th),
                                                 bsz=1, seq=th, tm=th)
    q_m, k_m, vt_m, u_m, gate_m, _ = inproj(x2d, f_meta, tri=tri(tm), bsz=bsz, seq=seq, tm=tm)

    attn_m, attn_h = _attention(q_m, k_m, vt_m, q_h, k_h, vt_h, bsz=bsz, seq=seq, tm=th, tq=tq)

    lbr, lbi, bbr, bbi = _s5_prep(lam_re, lam_im, log_dt, b_re, b_im)
    bblk, cblk = _s5_block_weights(bbr.reshape(SSM_GROUPS, SSM_STATE, SSM_GROUP),
                                   bbi.reshape(SSM_GROUPS, SSM_STATE, SSM_GROUP), c_re, c_im)
    lamr = lbr.reshape(1, SSM_GROUPS * SSM_STATE)
    lami = lbi.reshape(1, SSM_GROUPS * SSM_STATE)
    dsk = d_skip.astype(F32).reshape(1, SSM_W)
    u_meta = jnp.pad(u_h[th - tc:], ((0, 0), (0, (S5_META_BATCH - 1) * SSM_W)))
    z_meta, st_meta = _s5(u_meta, bblk, cblk, lamr, lami, dsk,
                          jnp.zeros((S5_META_BATCH, STATE_W), F32), nb=S5_META_BATCH, tc=tc, nch=1)
    z_h = jnp.concatenate([jnp.zeros((th - tc, SSM_W), BF16), z_meta[:, :SSM_W]], axis=0)
    z_m, _ = _s5(u_m, bblk, cblk, lamr, lami, dsk, jnp.broadcast_to(st_meta[0:1], (bsz, STATE_W)),
                 nb=bsz, tc=tc, nch=S5_CHUNKS_PER_STEP)

    cw = jnp.pad(conv_w.astype(F32), ((0, SUBLANES - CONV_W), (0, 0)))
    mixffn = functools.partial(
        _mixffn, wao=w_attn_out.astype(BF16), wglu=w_glu.astype(BF16), wo=w_o.astype(BF16),
        g2=norm_ffn_g.reshape(1, D_MODEL), wup=w_up.astype(BF16), cw=cw,
        cb=conv_b.astype(F32).reshape(1, 2 * D_FF), wdn=w_down.astype(BF16),
        g3=norm_final_g.reshape(1, D_MODEL))
    _, tail = mixffn(attn_h, z_h, gate_h, head, jnp.zeros((SUBLANES, 2 * D_FF), F32),
                     bsz=1, seq=th, tm=th, zero_rows=th - N_META)
    out, _ = mixffn(attn_m, z_m, gate_m, x2d, tail, bsz=bsz, seq=seq, tm=tm)
    return out.reshape(bsz, seq, D_MODEL)


def kernel(x, meta_tokens, norm_mix_g, w_in, b_forget, w_attn_out, ssm_lambda_re, ssm_lambda_im,
           ssm_b_re, ssm_b_im, ssm_c_re, ssm_c_im, ssm_d, ssm_log_dt, w_glu, w_o, norm_ffn_g,
           w_ffn_up, ffn_conv_w, ffn_conv_b, w_ffn_down, norm_final_g):
    assert norm_mix_g.shape[0] == 1, "single-layer block"
    return _forward(x, meta_tokens, norm_mix_g[0], w_in[0], b_forget[0], w_attn_out[0],
                    ssm_lambda_re[0], ssm_lambda_im[0], ssm_b_re[0], ssm_b_im[0], ssm_c_re[0],
                    ssm_c_im[0], ssm_d[0], ssm_log_dt[0], w_glu[0], w_o[0], norm_ffn_g[0],
                    w_ffn_up[0], ffn_conv_w[0], ffn_conv_b[0], w_ffn_down[0], norm_final_g,
                    tm=TM, tq=TQ, tc=S5_CHUNK)
```

```python
import functools
import math

import jax
import jax.numpy as jnp
import numpy as np
from jax import lax
from jax.experimental import pallas as pl
from jax.experimental.pallas import tpu as pltpu

F32 = jnp.float32
BF16 = jnp.bfloat16

D_MODEL = 1024
N_META = 16
HEADS = 8
HEAD_DIM = 64
ATTN_W = HEADS * HEAD_DIM
SSM_W = 512
SSM_GROUP = 16
SSM_GROUPS = SSM_W // SSM_GROUP
SSM_STATE = 64
STATE_W = 2 * SSM_GROUPS * SSM_STATE
HALF_GROUPS = SSM_GROUPS // 2
HALF_U = HALF_GROUPS * SSM_GROUP
HALF_STATE = STATE_W // 2
D_FF = 2816
CONV_W = 3
RMS_EPS = 1e-6

LANES = 128
SUBLANES = 8
BF16_ROWS = 16
PAIR_W = 2 * HEAD_DIM
N_PAIRS = HEADS // 2
ACC_ROWS = HEAD_DIM + BF16_ROWS
META_BLK = 128
NEG = -1e30
LOG2E = math.log2(math.e)

TM = 512
TQ = 256
S5_CHUNK = 32
S5_CHUNKS_PER_STEP = 2
S5_META_BATCH = 16
FF_CHUNK = 256
VMEM_LIMIT = 56 * 1024 * 1024


def _const_spec(shape):
    nd = len(shape)
    return pl.BlockSpec(shape, lambda *_: (0,) * nd, pipeline_mode=pl.Buffered(1))


def _params(n_grid):
    return pltpu.CompilerParams(dimension_semantics=("arbitrary",) * n_grid,
                                vmem_limit_bytes=VMEM_LIMIT)


def _rms(x, g):
    ms = jnp.mean(x * x, axis=-1, keepdims=True)
    return x * lax.rsqrt(ms + RMS_EPS) * g


def _dot(a, b):
    return jnp.dot(a, b, preferred_element_type=F32)


def _dot_nt(a, b):
    return lax.dot_general(a, b, (((1,), (1,)), ((), ())), preferred_element_type=F32)


def _pieces3(v):
    a = v.astype(BF16).astype(F32)
    r = v - a
    b = r.astype(BF16).astype(F32)
    c = (r - b).astype(BF16).astype(F32)
    return a, b, c


def _pack_pieces(v):
    a, b, c = _pieces3(v)
    return (a + pltpu.roll(b, HEADS, 1) + pltpu.roll(c, 2 * HEADS, 1)).astype(BF16)


def _unpack_sum(p):
    return p + pltpu.roll(p, LANES - HEADS, 1) + pltpu.roll(p, LANES - 2 * HEADS, 1)


def _inproj_kernel(x_ref, g_ref, wq_ref, wk_ref, wvt_ref, wu_ref, wg_ref, wf_ref, bf_ref,
                   tri_ref, place_ref, ones_ref, cin_ref,
                   q_ref, k_ref, vt_ref, u_ref, gate_ref, cout_ref, carry_sc, *, nt):
    i = pl.program_id(0)

    @pl.when(i % nt == 0)
    def _():
        carry_sc[...] = cin_ref[...]

    xh = _rms(x_ref[...], g_ref[...]).astype(BF16)
    tm = xh.shape[0]

    lane = lax.broadcasted_iota(jnp.int32, (tm, LANES), 1)
    fr = _dot(xh, wf_ref[...])
    f = fr + pltpu.roll(fr, LANES - HEADS, 1) + bf_ref[...]
    logf = jnp.where(lane < HEADS, jnp.minimum(f, 0.0) - jnp.log1p(jnp.exp(-jnp.abs(f))), 0.0)
    cum = _unpack_sum(_dot(tri_ref[...], _pack_pieces(logf))) + carry_sc[0:1, :]
    cum = jnp.where(lane < HEADS, cum, 0.0)
    carry_sc[...] = jnp.broadcast_to(cum[tm - 1:tm, :], carry_sc.shape)
    cout_ref[...] = carry_sc[...]

    vt_ref[...] = lax.dot_general(wvt_ref[...], xh, (((1,), (1,)), ((), ())),
                                  preferred_element_type=F32).astype(BF16)
    u_ref[...] = _dot(xh, wu_ref[...]).astype(BF16)
    gate_ref[...] = jax.nn.sigmoid(_dot(xh, wg_ref[...])).astype(BF16)

    aug = _dot(_pack_pieces(cum * LOG2E), place_ref[...]) + ones_ref[...]
    qr = _dot(xh, wq_ref[...]) * (LOG2E / math.sqrt(HEAD_DIM))
    kr = _dot(xh, wk_ref[...])
    for res, out_ref, aug_lo in ((qr, q_ref, 0), (kr, k_ref, HEADS * LANES)):
        for h in range(HEADS):
            src = res[:, (h // 2) * LANES:(h // 2 + 1) * LANES]
            if h % 2:
                src = pltpu.roll(src, HEAD_DIM, 1)
            blk = jnp.where(lane < HEAD_DIM, src, aug[:, aug_lo + h * LANES:aug_lo + (h + 1) * LANES])
            out_ref[:, h * LANES:(h + 1) * LANES] = blk.astype(BF16)


def _inproj(x2d, carry_in, g1, wq, wk, wvt, wu, wg, wf, bfp, tri, place, ones, *, bsz, seq, tm):
    nt = seq // tm
    n_tiles = bsz * nt
    rows = bsz * seq
    in_specs = [
        pl.BlockSpec((tm, D_MODEL), lambda i: (i, 0)),
        _const_spec((1, D_MODEL)),
        _const_spec(wq.shape),
        _const_spec(wk.shape),
        _const_spec(wvt.shape),
        _const_spec(wu.shape),
        _const_spec(wg.shape),
        _const_spec(wf.shape),
        _const_spec((1, LANES)),
        _const_spec((tm, tm)),
        _const_spec(place.shape),
        _const_spec(ones.shape),
        _const_spec((SUBLANES, LANES)),
    ]
    out_specs = [
        pl.BlockSpec((tm, HEADS * LANES), lambda i: (i, 0)),
        pl.BlockSpec((tm, HEADS * LANES), lambda i: (i, 0)),
        pl.BlockSpec((ATTN_W, tm), lambda i: (0, i)),
        pl.BlockSpec((tm, SSM_W), lambda i: (i % nt, i // nt)),
        pl.BlockSpec((tm, 2 * D_MODEL), lambda i: (i, 0)),
        pl.BlockSpec((SUBLANES, LANES), lambda i: (0, 0)),
    ]
    out_shape = [
        jax.ShapeDtypeStruct((rows, HEADS * LANES), BF16),
        jax.ShapeDtypeStruct((rows, HEADS * LANES), BF16),
        jax.ShapeDtypeStruct((ATTN_W, rows), BF16),
        jax.ShapeDtypeStruct((seq, bsz * SSM_W), BF16),
        jax.ShapeDtypeStruct((rows, 2 * D_MODEL), BF16),
        jax.ShapeDtypeStruct((SUBLANES, LANES), F32),
    ]
    return pl.pallas_call(
        functools.partial(_inproj_kernel, nt=nt),
        grid=(n_tiles,),
        in_specs=in_specs,
        out_specs=out_specs,
        out_shape=out_shape,
        scratch_shapes=[pltpu.VMEM((SUBLANES, LANES), F32)],
        compiler_params=_params(1),
        name="inproj",
    )(x2d, g1, wq, wk, wvt, wu, wg, wf, bfp, tri, place, ones, carry_in)


def _forget_lane_weights():
    place = np.zeros((LANES, 2 * HEADS * LANES), np.float32)
    ones = np.zeros((1, 2 * HEADS * LANES), np.float32)
    k_off = HEADS * LANES
    for piece in range(3):
        for h in range(HEADS):
            place[piece * HEADS + h, h * LANES + HEAD_DIM + piece] = 1.0
            place[piece * HEADS + h, k_off + h * LANES + HEAD_DIM + 3 + piece] = -1.0
            ones[0, h * LANES + HEAD_DIM + 3 + piece] = 1.0
            ones[0, k_off + h * LANES + HEAD_DIM + piece] = 1.0
    return jnp.asarray(place, BF16), jnp.asarray(ones)


def _attn_scores(k, q):
    return _dot_nt(k, q)


def _attn_update(h, blocks, m_sc, acc_sc, smax=None):
    rows = slice(h * ACC_ROWS, (h + 1) * ACC_ROWS)
    m_old = m_sc[h:h + 1, :]
    if smax is None:
        m_new = m_old
        for s, _ in blocks:
            m_new = jnp.maximum(m_new, jnp.max(s, axis=0, keepdims=True))
    else:
        m_new = jnp.maximum(m_old, smax)
    alpha = jnp.exp2(m_old - m_new)
    pv = None
    for s, vt in blocks:
        p = jnp.exp2(s - m_new).astype(BF16)
        lhs = jnp.concatenate([vt, jnp.ones((BF16_ROWS, vt.shape[1]), BF16)], axis=0)
        d = _dot(lhs, p)
        pv = d if pv is None else pv + d
    acc_sc[rows, :] = alpha * acc_sc[rows, :] + pv
    m_sc[h:h + 1, :] = m_new


def _attn_init(m_sc, acc_sc):
    m_sc[...] = jnp.full_like(m_sc, NEG)
    acc_sc[...] = jnp.zeros_like(acc_sc)


def _attn_finish(o_ref, acc_sc):
    for pair in range(N_PAIRS):
        halves = []
        for h in (2 * pair, 2 * pair + 1):
            base = h * ACC_ROWS
            inv = 1.0 / acc_sc[base + HEAD_DIM:base + HEAD_DIM + 1, :]
            halves.append(acc_sc[base:base + HEAD_DIM, :] * inv)
        ot = jnp.concatenate(halves, axis=0)
        o_ref[:, pair * PAIR_W:(pair + 1) * PAIR_W] = jnp.transpose(ot).astype(o_ref.dtype)


def _attn_main_kernel(q_ref, k_ref, vt_ref, km_ref, vtm_ref, o_ref, m_sc, acc_sc,
                      s0_sc, s1_sc, c0_sc, c1_sc, sm_sc):
    qi = pl.program_id(1)
    tq = q_ref.shape[0]
    _attn_init(m_sc, acc_sc)

    def scores(kv, s_sc, c_sc):
        off = pl.multiple_of(kv * tq, tq)
        for h in range(HEADS):
            lanes = slice(h * LANES, (h + 1) * LANES)
            s = _attn_scores(k_ref[pl.ds(off, tq), lanes], q_ref[:, lanes])
            s_sc[h] = s
            c_sc[h:h + 1, :] = jnp.max(s, axis=0, keepdims=True)

    def update(kv, s_sc, c_sc, diagonal):
        off = pl.multiple_of(kv * tq, tq)
        if diagonal:
            key = lax.broadcasted_iota(jnp.int32, (tq, tq), 0)
            qry = lax.broadcasted_iota(jnp.int32, (tq, tq), 1)
        for h in range(HEADS):
            vt = vt_ref[h * HEAD_DIM:(h + 1) * HEAD_DIM, pl.ds(off, tq)]
            if diagonal:
                s = jnp.where(key <= qry, s_sc[h], NEG)
                _attn_update(h, [(s, vt), (sm_sc[h], vtm_ref[h * HEAD_DIM:(h + 1) * HEAD_DIM, :])],
                             m_sc, acc_sc)
            else:
                _attn_update(h, [(s_sc[h], vt)], m_sc, acc_sc, smax=c_sc[h:h + 1, :])

    for h in range(HEADS):
        lanes = slice(h * LANES, (h + 1) * LANES)
        sm_sc[h] = _attn_scores(km_ref[:, lanes], q_ref[:, lanes])
    scores(0, s0_sc, c0_sc)

    def body(jj, carry):
        j = 2 * jj
        scores(j + 1, s1_sc, c1_sc)
        update(j, s0_sc, c0_sc, False)
        scores(j + 2, s0_sc, c0_sc)
        update(j + 1, s1_sc, c1_sc, False)
        return carry

    lax.fori_loop(0, qi // 2, body, 0)

    @pl.when(qi % 2 == 0)
    def _():
        update(qi, s0_sc, c0_sc, True)

    @pl.when(qi % 2 == 1)
    def _():
        scores(qi, s1_sc, c1_sc)
        update(qi - 1, s0_sc, c0_sc, False)
        update(qi, s1_sc, c1_sc, True)

    _attn_finish(o_ref, acc_sc)


def _attn_head_kernel(q_ref, km_ref, vtm_ref, o_ref, m_sc, acc_sc):
    tq = q_ref.shape[0]
    _attn_init(m_sc, acc_sc)
    key = lax.broadcasted_iota(jnp.int32, (N_META, tq), 0)
    qry = lax.broadcasted_iota(jnp.int32, (N_META, tq), 1)
    mask = key <= qry - (tq - N_META)
    for h in range(HEADS):
        lanes = slice(h * LANES, (h + 1) * LANES)
        s = jnp.where(mask, _attn_scores(km_ref[:, lanes], q_ref[:, lanes]), NEG)
        _attn_update(h, [(s, vtm_ref[h * HEAD_DIM:(h + 1) * HEAD_DIM, :])], m_sc, acc_sc)
    _attn_finish(o_ref, acc_sc)


def _attn_scratch(tq):
    return [pltpu.VMEM((HEADS, tq), F32), pltpu.VMEM((HEADS * ACC_ROWS, tq), F32)]


def _attention(q_m, k_m, vt_m, q_h, k_h, vt_h, *, bsz, seq, tm, tq):
    nq = seq // tq
    hm = tm // META_BLK
    k_meta = k_h[tm - N_META:]
    vt_meta = vt_h[:, tm - N_META:]

    attn_main = pl.pallas_call(
        _attn_main_kernel,
        grid=(bsz, nq),
        in_specs=[
            pl.BlockSpec((tq, HEADS * LANES), lambda b, qi: (b * nq + qi, 0)),
            pl.BlockSpec((seq, HEADS * LANES), lambda b, qi: (b, 0)),
            pl.BlockSpec((ATTN_W, seq), lambda b, qi: (0, b)),
            pl.BlockSpec(k_meta.shape, lambda b, qi: (0, 0)),
            pl.BlockSpec(vt_meta.shape, lambda b, qi: (0, 0)),
        ],
        out_specs=pl.BlockSpec((tq, ATTN_W), lambda b, qi: (b * nq + qi, 0)),
        out_shape=jax.ShapeDtypeStruct((bsz * seq, ATTN_W), BF16),
        scratch_shapes=_attn_scratch(tq) + [pltpu.VMEM((HEADS, tq, tq), F32),
                                            pltpu.VMEM((HEADS, tq, tq), F32),
                                            pltpu.VMEM((HEADS, tq), F32),
                                            pltpu.VMEM((HEADS, tq), F32),
                                            pltpu.VMEM((HEADS, N_META, tq), F32)],
        compiler_params=_params(2),
        name="attn_main",
    )(q_m, k_m, vt_m, k_meta, vt_meta)

    attn_head = pl.pallas_call(
        _attn_head_kernel,
        grid=(1,),
        in_specs=[
            pl.BlockSpec((META_BLK, HEADS * LANES), lambda i: (hm - 1, 0)),
            pl.BlockSpec(k_meta.shape, lambda i: (0, 0)),
            pl.BlockSpec(vt_meta.shape, lambda i: (0, 0)),
        ],
        out_specs=pl.BlockSpec((META_BLK, ATTN_W), lambda i: (0, 0)),
        out_shape=jax.ShapeDtypeStruct((META_BLK, ATTN_W), BF16),
        scratch_shapes=_attn_scratch(META_BLK),
        compiler_params=_params(1),
        name="attn_head",
    )(q_h, k_meta, vt_meta)
    return attn_main, attn_head


def _s5_prep_kernel(lre_ref, lim_ref, ldt_ref, bre_ref, bim_ref, lbr_ref, lbi_ref, bbr_ref, bbi_ref):
    lre = lre_ref[...]
    lim = lim_ref[...]
    dt = jnp.exp(ldt_ref[...])
    mag = jnp.exp(lre * dt)
    ang = lim * dt
    lbr = mag * jnp.cos(ang)
    lbi = mag * jnp.sin(ang)
    lbr_ref[...] = lbr
    lbi_ref[...] = lbi
    nr = lbr - 1.0
    ni = lbi
    den = lre * lre + lim * lim
    cr = (nr * lre + ni * lim) / den
    ci = (ni * lre - nr * lim) / den
    bre = bre_ref[...]
    bim = bim_ref[...]
    bbr_ref[...] = cr * bre - ci * bim
    bbi_ref[...] = cr * bim + ci * bre


def _s5_prep(lam_re, lam_im, log_dt, b_re, b_im):
    n = SSM_GROUPS * SSM_STATE
    col = lambda a: a.reshape(n, 1).astype(F32)
    ldt = jnp.broadcast_to(log_dt.astype(F32)[:, None], (SSM_GROUPS, SSM_STATE))
    return pl.pallas_call(
        _s5_prep_kernel,
        out_shape=[jax.ShapeDtypeStruct((n, 1), F32), jax.ShapeDtypeStruct((n, 1), F32),
                   jax.ShapeDtypeStruct((n, SSM_GROUP), F32), jax.ShapeDtypeStruct((n, SSM_GROUP), F32)],
        name="s5_prep",
    )(col(lam_re), col(lam_im), col(ldt), b_re.reshape(n, SSM_GROUP).astype(F32),
      b_im.reshape(n, SSM_GROUP).astype(F32))


def _gelu_tanh(x):
    c = math.sqrt(2.0 / math.pi)
    return 0.5 * x * (1.0 + jnp.tanh(c * (x + 0.044715 * (x * x * x))))


def _s5_kernel(u_ref, bblk_ref, cblk_ref, lamr_ref, lami_ref, d_ref, init_ref, z_ref, fin_ref,
               utb_sc, st_sc, ztb_sc, carry_sc, *, nb, tc, nch):
    c = pl.program_id(0)
    half_re = HALF_STATE // 2
    lane_chunk = 256
    n_lt = SSM_W // LANES

    @pl.when(c == 0)
    def _():
        carry_sc[...] = init_ref[...]

    def scan_lanes(ch, h, j, ub):
        lam_lo = h * half_re + j * lane_chunk
        lr = jnp.broadcast_to(lamr_ref[:, lam_lo:lam_lo + lane_chunk], (nb, lane_chunk))
        li = jnp.broadcast_to(lami_ref[:, lam_lo:lam_lo + lane_chunk], (nb, lane_chunk))
        re_l = slice(j * lane_chunk, (j + 1) * lane_chunk)
        im_l = slice(re_l.start + half_re, re_l.stop + half_re)
        re = slice(h * HALF_STATE + re_l.start, h * HALF_STATE + re_l.stop)
        im = slice(h * HALF_STATE + im_l.start, h * HALF_STATE + im_l.stop)
        bre = _dot_nt(ub, bblk_ref[h, re_l, :])
        bim = _dot_nt(ub, bblk_ref[h, im_l, :])
        sre, sim = carry_sc[:, re], carry_sc[:, im]
        for t in range(tc):
            rows = slice(t * nb, (t + 1) * nb)
            nre = lr * sre - li * sim + bre[rows]
            nim = lr * sim + li * sre + bim[rows]
            st_sc[ch, rows, re] = nre.astype(BF16)
            st_sc[ch, rows, im] = nim.astype(BF16)
            sre, sim = nre, nim
        carry_sc[:, re] = sre
        carry_sc[:, im] = sim

    for ch in range(nch):
        for b in range(nb):
            for j in range(n_lt):
                lo = b * SSM_W + j * LANES
                utb_sc[ch * n_lt + j, pl.ds(b, tc, stride=nb), :] = (
                    u_ref[ch * tc:(ch + 1) * tc, lo:lo + LANES].astype(F32))
        for h in range(2):
            ub = jnp.concatenate([utb_sc[ch * n_lt + 2 * h], utb_sc[ch * n_lt + 2 * h + 1]],
                                 axis=1).astype(BF16)
            for j in range(half_re // lane_chunk):
                scan_lanes(ch, h, j, ub)
        ys = [_dot_nt(st_sc[ch, :, h * HALF_STATE:(h + 1) * HALF_STATE], cblk_ref[h]) for h in range(2)]
        for j in range(n_lt):
            y = (ys[j // 2][:, (j % 2) * LANES:(j % 2 + 1) * LANES]
                 + d_ref[:, j * LANES:(j + 1) * LANES] * utb_sc[ch * n_lt + j])
            ztb_sc[ch * n_lt + j] = _gelu_tanh(y)
        for b in range(nb):
            for j in range(n_lt):
                lo = b * SSM_W + j * LANES
                z_ref[ch * tc:(ch + 1) * tc, lo:lo + LANES] = (
                    ztb_sc[ch * n_lt + j, pl.ds(b, tc, stride=nb), :].astype(z_ref.dtype))

    fin_ref[...] = carry_sc[...]


def _s5(u_tm, bblk, cblk, lamr, lami, dskip, init, *, nb, tc, nch):
    steps, width = u_tm.shape
    blk = tc * nch
    n_lt = SSM_W // LANES
    assert width == nb * SSM_W and steps % blk == 0 and init.shape == (nb, STATE_W)
    return pl.pallas_call(
        functools.partial(_s5_kernel, nb=nb, tc=tc, nch=nch),
        grid=(steps // blk,),
        in_specs=[
            pl.BlockSpec((blk, width), lambda c: (c, 0)),
            _const_spec(bblk.shape),
            _const_spec(cblk.shape),
            _const_spec(lamr.shape),
            _const_spec(lami.shape),
            _const_spec(dskip.shape),
            _const_spec(init.shape),
        ],
        out_specs=[pl.BlockSpec((blk, width), lambda c: (c, 0)),
                   pl.BlockSpec((nb, STATE_W), lambda c: (0, 0))],
        out_shape=[jax.ShapeDtypeStruct((steps, width), BF16),
                   jax.ShapeDtypeStruct((nb, STATE_W), F32)],
        scratch_shapes=[pltpu.VMEM((nch * n_lt, tc * nb, LANES), F32),
                        pltpu.VMEM((nch, tc * nb, STATE_W), BF16),
                        pltpu.VMEM((nch * n_lt, tc * nb, LANES), F32),
                        pltpu.VMEM((nb, STATE_W), F32)],
        compiler_params=_params(1),
        name="s5_scan",
    )(u_tm, bblk, cblk, lamr, lami, dskip, init)


def _s5_block_weights(bbr, bbi, c_re, c_im):
    eye = jnp.eye(HALF_GROUPS, dtype=F32)
    bb = jnp.stack([bbr, bbi]).reshape(2, 2, HALF_GROUPS, SSM_STATE, 1, SSM_GROUP)
    bblk_t = jnp.swapaxes(bb * eye[None, None, :, None, :, None], 0, 1)
    cc = jnp.stack([c_re.astype(F32), -c_im.astype(F32)])
    cc = cc.reshape(2, 2, HALF_GROUPS, SSM_GROUP, 1, SSM_STATE)
    cblk_t = jnp.transpose(cc * eye[None, None, :, None, :, None], (1, 2, 3, 0, 4, 5))
    return (bblk_t.reshape(2, HALF_STATE, HALF_U).astype(BF16),
            cblk_t.reshape(2, HALF_U, HALF_STATE).astype(BF16))


def _mixffn_kernel(attn_ref, z_ref, gate_ref, x_ref, wao_ref, wglu_ref, wo_ref, g2_ref,
                   wup_ref, cw_ref, cb_ref, wdn_ref, g3_ref, tin_ref,
                   out_ref, tout_ref, carry_sc, act_sc, h1_sc, *, nt, zero_rows):
    i = pl.program_id(0)

    @pl.when(i % nt == 0)
    def _():
        carry_sc[...] = tin_ref[...]

    ya = _dot(attn_ref[...], wao_ref[...])
    zg = _dot(z_ref[...], wglu_ref[...])
    ys = zg[:, :D_MODEL] * jax.nn.sigmoid(zg[:, D_MODEL:])
    g = gate_ref[...].astype(F32)
    merged = g[:, :D_MODEL] * ya + g[:, D_MODEL:] * ys
    h1 = x_ref[...] + _dot(merged.astype(BF16), wo_ref[...])
    tm = h1.shape[0]
    if zero_rows:
        r = lax.broadcasted_iota(jnp.int32, (tm, 1), 0)
        h1 = jnp.where(r < zero_rows, 0.0, h1)
    h1_sc[...] = h1
    xn = _rms(h1, g2_ref[...]).astype(BF16)

    row = lax.broadcasted_iota(jnp.int32, (SUBLANES, 1), 0)
    for c in range(D_FF // FF_CHUNK):
        conv = []
        for part in range(2):
            lo = part * D_FF + c * FF_CHUNK
            cols = slice(lo, lo + FF_CHUNK)
            h = _dot(xn, wup_ref[:, cols])
            prev = carry_sc[:, cols]
            p1 = prev[SUBLANES - 1:SUBLANES, :]
            p2 = prev[SUBLANES - 2:SUBLANES - 1, :]
            r1 = pltpu.roll(h, 1, 0)
            r2 = pltpu.roll(h, 2, 0)
            t1 = jnp.where(row == 0, p1, r1[:SUBLANES])
            t2 = jnp.where(row == 0, p2, jnp.where(row == 1, p1, r2[:SUBLANES]))
            hm1 = jnp.concatenate([t1, r1[SUBLANES:]], axis=0)
            hm2 = jnp.concatenate([t2, r2[SUBLANES:]], axis=0)
            w = cw_ref[:, cols]
            conv.append(cb_ref[:, cols] + (w[0:1, :] * hm2 + w[1:2, :] * hm1 + w[2:3, :] * h))
            carry_sc[:, cols] = h[tm - SUBLANES:tm, :]
        val, gate = conv
        act_sc[:, c * FF_CHUNK:(c + 1) * FF_CHUNK] = ((gate * jax.nn.sigmoid(gate)) * val).astype(BF16)

    tout_ref[...] = carry_sc[...]
    out_ref[...] = _rms(h1_sc[...] + _dot(act_sc[...], wdn_ref[...]), g3_ref[...])


def _mixffn(attn, z_tm, gate, x2d, tail_in, wao, wglu, wo, g2, wup, cw, cb, wdn, g3, *,
            bsz, seq, tm, zero_rows=0):
    nt = seq // tm
    return pl.pallas_call(
        functools.partial(_mixffn_kernel, nt=nt, zero_rows=zero_rows),
        grid=(bsz * nt,),
        in_specs=[
            pl.BlockSpec((tm, ATTN_W), lambda i: (i, 0)),
            pl.BlockSpec((tm, SSM_W), lambda i: (i % nt, i // nt)),
            pl.BlockSpec((tm, 2 * D_MODEL), lambda i: (i, 0)),
            pl.BlockSpec((tm, D_MODEL), lambda i: (i, 0)),
            _const_spec(wao.shape),
            _const_spec(wglu.shape),
            _const_spec(wo.shape),
            _const_spec((1, D_MODEL)),
            _const_spec(wup.shape),
            _const_spec(cw.shape),
            _const_spec(cb.shape),
            _const_spec(wdn.shape),
            _const_spec((1, D_MODEL)),
            _const_spec(tail_in.shape),
        ],
        out_specs=[pl.BlockSpec((tm, D_MODEL), lambda i: (i, 0)),
                   pl.BlockSpec(tail_in.shape, lambda i: (0, 0))],
        out_shape=[jax.ShapeDtypeStruct((bsz * seq, D_MODEL), F32),
                   jax.ShapeDtypeStruct(tail_in.shape, F32)],
        scratch_shapes=[pltpu.VMEM(tail_in.shape, F32), pltpu.VMEM((tm, D_FF), BF16),
                        pltpu.VMEM((tm, D_MODEL), F32)],
        compiler_params=_params(1),
        name="mixffn",
    )(attn, z_tm, gate, x2d, wao, wglu, wo, g2, wup, cw, cb, wdn, g3, tail_in)


def _forward(x, meta_tokens, norm_mix_g, w_in, b_forget, w_attn_out, lam_re, lam_im, b_re, b_im,
             c_re, c_im, d_skip, log_dt, w_glu, w_o, norm_ffn_g, w_up, conv_w, conv_b, w_down,
             norm_final_g, *, tm, tq, tc):
    bsz, seq, _ = x.shape
    assert seq % tm == 0 and seq % tq == 0
    assert seq % (tc * S5_CHUNKS_PER_STEP) == 0
    th = META_BLK
    assert th >= tc >= N_META
    x2d = x.reshape(bsz * seq, D_MODEL)
    head = jnp.concatenate([jnp.zeros((th - N_META, D_MODEL), F32), meta_tokens.astype(F32)], axis=0)

    q_end, k_end, v_end = ATTN_W, 2 * ATTN_W, 3 * ATTN_W
    f_end = v_end + HEADS
    u_end = f_end + SSM_W
    wq = w_in[:, :q_end].astype(BF16)
    wk = w_in[:, q_end:k_end].astype(BF16)
    wvt = jnp.transpose(w_in[:, k_end:v_end]).astype(BF16)
    place, ones = _forget_lane_weights()
    wf32 = w_in[:, v_end:f_end]
    wf_hi = wf32.astype(BF16)
    wf_lo = (wf32 - wf_hi.astype(F32)).astype(BF16)
    wf = jnp.pad(jnp.concatenate([wf_hi, wf_lo], axis=1), ((0, 0), (0, LANES - 2 * HEADS)))
    bfp = jnp.pad(b_forget.astype(F32), (0, LANES - HEADS)).reshape(1, LANES)
    wu = w_in[:, f_end:u_end].astype(BF16)
    wg = w_in[:, u_end:].astype(BF16)
    tri = lambda n: jnp.asarray(np.tril(np.ones((n, n), np.float32)), BF16)
    g1 = norm_mix_g.reshape(1, D_MODEL)
    inproj = functools.partial(_inproj, g1=g1, wq=wq, wk=wk, wvt=wvt, wu=wu, wg=wg, wf=wf,
                               bfp=bfp, place=place, ones=ones)
    q_h, k_h, vt_h, u_h, gate_h, f_meta = inproj(head, jnp.zeros((SUBLANES, LANES), F32), tri=tri(th),
                                                 bsz=1, seq=th, tm=th)
    q_m, k_m, vt_m, u_m, gate_m, _ = inproj(x2d, f_meta, tri=tri(tm), bsz=bsz, seq=seq, tm=tm)

    attn_m, attn_h = _attention(q_m, k_m, vt_m, q_h, k_h, vt_h, bsz=bsz, seq=seq, tm=th, tq=tq)

    lbr, lbi, bbr, bbi = _s5_prep(lam_re, lam_im, log_dt, b_re, b_im)
    bblk, cblk = _s5_block_weights(bbr.reshape(SSM_GROUPS, SSM_STATE, SSM_GROUP),
                                   bbi.reshape(SSM_GROUPS, SSM_STATE, SSM_GROUP), c_re, c_im)
    lamr = lbr.reshape(1, SSM_GROUPS * SSM_STATE)
    lami = lbi.reshape(1, SSM_GROUPS * SSM_STATE)
    dsk = d_skip.astype(F32).reshape(1, SSM_W)
    u_meta = jnp.pad(u_h[th - tc:], ((0, 0), (0, (S5_META_BATCH - 1) * SSM_W)))
    z_meta, st_meta = _s5(u_meta, bblk, cblk, lamr, lami, dsk,
                          jnp.zeros((S5_META_BATCH, STATE_W), F32), nb=S5_META_BATCH, tc=tc, nch=1)
    z_h = jnp.concatenate([jnp.zeros((th - tc, SSM_W), BF16), z_meta[:, :SSM_W]], axis=0)
    z_m, _ = _s5(u_m, bblk, cblk, lamr, lami, dsk, jnp.broadcast_to(st_meta[0:1], (bsz, STATE_W)),
                 nb=bsz, tc=tc, nch=S5_CHUNKS_PER_STEP)

    cw = jnp.pad(conv_w.astype(F32), ((0, SUBLANES - CONV_W), (0, 0)))
    mixffn = functools.partial(
        _mixffn, wao=w_attn_out.astype(BF16), wglu=w_glu.astype(BF16), wo=w_o.astype(BF16),
        g2=norm_ffn_g.reshape(1, D_MODEL), wup=w_up.astype(BF16), cw=cw,
        cb=conv_b.astype(F32).reshape(1, 2 * D_FF), wdn=w_down.astype(BF16),
        g3=norm_final_g.reshape(1, D_MODEL))
    _, tail = mixffn(attn_h, z_h, gate_h, head, jnp.zeros((SUBLANES, 2 * D_FF), F32),
                     bsz=1, seq=th, tm=th, zero_rows=th - N_META)
    out, _ = mixffn(attn_m, z_m, gate_m, x2d, tail, bsz=bsz, seq=seq, tm=tm)
    return out.reshape(bsz, seq, D_MODEL)


def kernel(x, meta_tokens, norm_mix_g, w_in, b_forget, w_attn_out, ssm_lambda_re, ssm_lambda_im,
           ssm_b_re, ssm_b_im, ssm_c_re, ssm_c_im, ssm_d, ssm_log_dt, w_glu, w_o, norm_ffn_g,
           w_ffn_up, ffn_conv_w, ffn_conv_b, w_ffn_down, norm_final_g):
    assert norm_mix_g.shape[0] == 1, "single-layer block"
    return _forward(x, meta_tokens, norm_mix_g[0], w_in[0], b_forget[0], w_attn_out[0],
                    ssm_lambda_re[0], ssm_lambda_im[0], ssm_b_re[0], ssm_b_im[0], ssm_c_re[0],
                    ssm_c_im[0], ssm_d[0], ssm_log_dt[0], w_glu[0], w_o[0], norm_ffn_g[0],
                    w_ffn_up[0], ffn_conv_w[0], ffn_conv_b[0], w_ffn_down[0], norm_final_g,
                    tm=TM, tq=TQ, tc=S5_CHUNK)
```

```python
import functools
import math

import jax
import jax.numpy as jnp
import numpy as np
from jax import lax
from jax.experimental import pallas as pl
from jax.experimental.pallas import tpu as pltpu

F32 = jnp.float32
BF16 = jnp.bfloat16

D_MODEL = 1024
N_META = 16
HEADS = 8
HEAD_DIM = 64
ATTN_W = HEADS * HEAD_DIM
SSM_W = 512
SSM_GROUP = 16
SSM_GROUPS = SSM_W // SSM_GROUP
SSM_STATE = 64
STATE_W = 2 * SSM_GROUPS * SSM_STATE
HALF_GROUPS = SSM_GROUPS // 2
HALF_U = HALF_GROUPS * SSM_GROUP
HALF_STATE = STATE_W // 2
D_FF = 2816
CONV_W = 3
RMS_EPS = 1e-6

LANES = 128
SUBLANES = 8
BF16_ROWS = 16
PAIR_W = 2 * HEAD_DIM
N_PAIRS = HEADS // 2
ACC_ROWS = HEAD_DIM + BF16_ROWS
META_BLK = 128
NEG = -1e30
LOG2E = math.log2(math.e)

TM = 512
TQ = 256
S5_CHUNK = 32
S5_CHUNKS_PER_STEP = 2
S5_META_BATCH = 16
FF_CHUNK = 256
VMEM_LIMIT = 56 * 1024 * 1024


def _const_spec(shape):
    nd = len(shape)
    return pl.BlockSpec(shape, lambda *_: (0,) * nd, pipeline_mode=pl.Buffered(1))


def _params(n_grid):
    return pltpu.CompilerParams(dimension_semantics=("arbitrary",) * n_grid,
                                vmem_limit_bytes=VMEM_LIMIT)


def _rms(x, g):
    ms = jnp.mean(x * x, axis=-1, keepdims=True)
    return x * lax.rsqrt(ms + RMS_EPS) * g


def _dot(a, b):
    return jnp.dot(a, b, preferred_element_type=F32)


def _dot_nt(a, b):
    return lax.dot_general(a, b, (((1,), (1,)), ((), ())), preferred_element_type=F32)


def _pieces3(v):
    a = v.astype(BF16).astype(F32)
    r = v - a
    b = r.astype(BF16).astype(F32)
    c = (r - b).astype(BF16).astype(F32)
    return a, b, c


def _pack_pieces(v):
    a, b, c = _pieces3(v)
    return (a + pltpu.roll(b, HEADS, 1) + pltpu.roll(c, 2 * HEADS, 1)).astype(BF16)


def _unpack_sum(p):
    return p + pltpu.roll(p, LANES - HEADS, 1) + pltpu.roll(p, LANES - 2 * HEADS, 1)


def _inproj_kernel(x_ref, g_ref, wq_ref, wk_ref, wvt_ref, wu_ref, wg_ref, wf_ref, bf_ref,
                   tri_ref, place_ref, ones_ref, cin_ref,
                   q_ref, k_ref, vt_ref, u_ref, gate_ref, cout_ref, carry_sc, *, nt):
    i = pl.program_id(0)

    @pl.when(i % nt == 0)
    def _():
        carry_sc[...] = cin_ref[...]

    xh = _rms(x_ref[...], g_ref[...]).astype(BF16)
    tm = xh.shape[0]

    lane = lax.broadcasted_iota(jnp.int32, (tm, LANES), 1)
    fr = _dot(xh, wf_ref[...])
    vt_ref[...] = _dot_nt(wvt_ref[...], xh).astype(BF16)
    u_ref[...] = _dot(xh, wu_ref[...]).astype(BF16)
    f = fr + pltpu.roll(fr, LANES - HEADS, 1) + bf_ref[...]
    logf = jnp.where(lane < HEADS, jnp.minimum(f, 0.0) - jnp.log1p(jnp.exp(-jnp.abs(f))), 0.0)
    packed = _pack_pieces(logf)
    gate_ref[...] = jax.nn.sigmoid(_dot(xh, wg_ref[...])).astype(BF16)
    cum = _unpack_sum(_dot(tri_ref[...], packed)) + carry_sc[0:1, :]
    cum = jnp.where(lane < HEADS, cum, 0.0)
    carry_sc[...] = jnp.broadcast_to(cum[tm - 1:tm, :], carry_sc.shape)
    cout_ref[...] = carry_sc[...]

    qr = _dot(xh, wq_ref[...]) * (LOG2E / math.sqrt(HEAD_DIM))
    kr = _dot(xh, wk_ref[...])
    aug = _dot(_pack_pieces(cum * LOG2E), place_ref[...]) + ones_ref[...]
    for res, out_ref, aug_lo in ((qr, q_ref, 0), (kr, k_ref, HEADS * LANES)):
        for h in range(HEADS):
            src = res[:, (h // 2) * LANES:(h // 2 + 1) * LANES]
            if h % 2:
                src = pltpu.roll(src, HEAD_DIM, 1)
            blk = jnp.where(lane < HEAD_DIM, src, aug[:, aug_lo + h * LANES:aug_lo + (h + 1) * LANES])
            out_ref[:, h * LANES:(h + 1) * LANES] = blk.astype(BF16)


def _inproj(x2d, carry_in, g1, wq, wk, wvt, wu, wg, wf, bfp, tri, place, ones, *, bsz, seq, tm):
    nt = seq // tm
    n_tiles = bsz * nt
    rows = bsz * seq
    in_specs = [
        pl.BlockSpec((tm, D_MODEL), lambda i: (i, 0)),
        _const_spec((1, D_MODEL)),
        _const_spec(wq.shape),
        _const_spec(wk.shape),
        _const_spec(wvt.shape),
        _const_spec(wu.shape),
        _const_spec(wg.shape),
        _const_spec(wf.shape),
        _const_spec((1, LANES)),
        _const_spec((tm, tm)),
        _const_spec(place.shape),
        _const_spec(ones.shape),
        _const_spec((SUBLANES, LANES)),
    ]
    out_specs = [
        pl.BlockSpec((tm, HEADS * LANES), lambda i: (i, 0)),
        pl.BlockSpec((tm, HEADS * LANES), lambda i: (i, 0)),
        pl.BlockSpec((ATTN_W, tm), lambda i: (0, i)),
        pl.BlockSpec((tm, SSM_W), lambda i: (i % nt, i // nt)),
        pl.BlockSpec((tm, 2 * D_MODEL), lambda i: (i, 0)),
        pl.BlockSpec((SUBLANES, LANES), lambda i: (0, 0)),
    ]
    out_shape = [
        jax.ShapeDtypeStruct((rows, HEADS * LANES), BF16),
        jax.ShapeDtypeStruct((rows, HEADS * LANES), BF16),
        jax.ShapeDtypeStruct((ATTN_W, rows), BF16),
        jax.ShapeDtypeStruct((seq, bsz * SSM_W), BF16),
        jax.ShapeDtypeStruct((rows, 2 * D_MODEL), BF16),
        jax.ShapeDtypeStruct((SUBLANES, LANES), F32),
    ]
    return pl.pallas_call(
        functools.partial(_inproj_kernel, nt=nt),
        grid=(n_tiles,),
        in_specs=in_specs,
        out_specs=out_specs,
        out_shape=out_shape,
        scratch_shapes=[pltpu.VMEM((SUBLANES, LANES), F32)],
        compiler_params=_params(1),
        name="inproj",
    )(x2d, g1, wq, wk, wvt, wu, wg, wf, bfp, tri, place, ones, carry_in)


def _forget_lane_weights():
    place = np.zeros((LANES, 2 * HEADS * LANES), np.float32)
    ones = np.zeros((1, 2 * HEADS * LANES), np.float32)
    k_off = HEADS * LANES
    for piece in range(3):
        for h in range(HEADS):
            place[piece * HEADS + h, h * LANES + HEAD_DIM + piece] = 1.0
            place[piece * HEADS + h, k_off + h * LANES + HEAD_DIM + 3 + piece] = -1.0
            ones[0, h * LANES + HEAD_DIM + 3 + piece] = 1.0
            ones[0, k_off + h * LANES + HEAD_DIM + piece] = 1.0
    return jnp.asarray(place, BF16), jnp.asarray(ones)


def _attn_scores(k, q):
    return _dot_nt(k, q)


def _attn_update(h, blocks, m_sc, acc_sc, smax=None):
    rows = slice(h * ACC_ROWS, (h + 1) * ACC_ROWS)
    m_old = m_sc[h:h + 1, :]
    if smax is None:
        m_new = m_old
        for s, _ in blocks:
            m_new = jnp.maximum(m_new, jnp.max(s, axis=0, keepdims=True))
    else:
        m_new = jnp.maximum(m_old, smax)
    alpha = jnp.exp2(m_old - m_new)
    pv = None
    for s, vt in blocks:
        p = jnp.exp2(s - m_new).astype(BF16)
        lhs = jnp.concatenate([vt, jnp.ones((BF16_ROWS, vt.shape[1]), BF16)], axis=0)
        d = _dot(lhs, p)
        pv = d if pv is None else pv + d
    acc_sc[rows, :] = alpha * acc_sc[rows, :] + pv
    m_sc[h:h + 1, :] = m_new


def _attn_init(m_sc, acc_sc):
    m_sc[...] = jnp.full_like(m_sc, NEG)
    acc_sc[...] = jnp.zeros_like(acc_sc)


def _attn_finish(o_ref, acc_sc):
    for pair in range(N_PAIRS):
        halves = []
        for h in (2 * pair, 2 * pair + 1):
            base = h * ACC_ROWS
            inv = 1.0 / acc_sc[base + HEAD_DIM:base + HEAD_DIM + 1, :]
            halves.append(acc_sc[base:base + HEAD_DIM, :] * inv)
        ot = jnp.concatenate(halves, axis=0)
        o_ref[:, pair * PAIR_W:(pair + 1) * PAIR_W] = jnp.transpose(ot).astype(o_ref.dtype)


def _attn_main_kernel(q_ref, k_ref, vt_ref, km_ref, vtm_ref, o_ref, m_sc, acc_sc,
                      s0_sc, s1_sc, c0_sc, c1_sc, sm_sc):
    qi = pl.program_id(1)
    tq = q_ref.shape[0]
    _attn_init(m_sc, acc_sc)

    def scores(kv, s_sc, c_sc, heads=range(HEADS)):
        off = pl.multiple_of(kv * tq, tq)
        for h in heads:
            lanes = slice(h * LANES, (h + 1) * LANES)
            s = _attn_scores(k_ref[pl.ds(off, tq), lanes], q_ref[:, lanes])
            s_sc[h] = s
            c_sc[h:h + 1, :] = jnp.max(s, axis=0, keepdims=True)

    def update(kv, s_sc, c_sc, diagonal, heads=range(HEADS)):
        off = pl.multiple_of(kv * tq, tq)
        if diagonal:
            key = lax.broadcasted_iota(jnp.int32, (tq, tq), 0)
            qry = lax.broadcasted_iota(jnp.int32, (tq, tq), 1)
        for h in heads:
            vt = vt_ref[h * HEAD_DIM:(h + 1) * HEAD_DIM, pl.ds(off, tq)]
            if diagonal:
                s = jnp.where(key <= qry, s_sc[h], NEG)
                _attn_update(h, [(s, vt), (sm_sc[h], vtm_ref[h * HEAD_DIM:(h + 1) * HEAD_DIM, :])],
                             m_sc, acc_sc)
            else:
                _attn_update(h, [(s_sc[h], vt)], m_sc, acc_sc, smax=c_sc[h:h + 1, :])

    for h in range(HEADS):
        lanes = slice(h * LANES, (h + 1) * LANES)
        sm_sc[h] = _attn_scores(km_ref[:, lanes], q_ref[:, lanes])
    scores(0, s0_sc, c0_sc)

    def overlapped(kv_next, s_next, c_next, kv, s_cur, c_cur):
        for h in range(HEADS):
            scores(kv_next, s_next, c_next, heads=(h,))
            update(kv, s_cur, c_cur, False, heads=(h,))

    def body(jj, carry):
        j = 2 * jj
        overlapped(j + 1, s1_sc, c1_sc, j, s0_sc, c0_sc)
        overlapped(j + 2, s0_sc, c0_sc, j + 1, s1_sc, c1_sc)
        return carry

    lax.fori_loop(0, qi // 2, body, 0)

    @pl.when(qi % 2 == 0)
    def _():
        update(qi, s0_sc, c0_sc, True)

    @pl.when(qi % 2 == 1)
    def _():
        overlapped(qi, s1_sc, c1_sc, qi - 1, s0_sc, c0_sc)
        update(qi, s1_sc, c1_sc, True)

    _attn_finish(o_ref, acc_sc)


def _attn_head_kernel(q_ref, km_ref, vtm_ref, o_ref, m_sc, acc_sc):
    tq = q_ref.shape[0]
    _attn_init(m_sc, acc_sc)
    key = lax.broadcasted_iota(jnp.int32, (N_META, tq), 0)
    qry = lax.broadcasted_iota(jnp.int32, (N_META, tq), 1)
    mask = key <= qry - (tq - N_META)
    for h in range(HEADS):
        lanes = slice(h * LANES, (h + 1) * LANES)
        s = jnp.where(mask, _attn_scores(km_ref[:, lanes], q_ref[:, lanes]), NEG)
        _attn_update(h, [(s, vtm_ref[h * HEAD_DIM:(h + 1) * HEAD_DIM, :])], m_sc, acc_sc)
    _attn_finish(o_ref, acc_sc)


def _attn_scratch(tq):
    return [pltpu.VMEM((HEADS, tq), F32), pltpu.VMEM((HEADS * ACC_ROWS, tq), F32)]


def _attention(q_m, k_m, vt_m, q_h, k_h, vt_h, *, bsz, seq, tm, tq):
    nq = seq // tq
    hm = tm // META_BLK
    k_meta = k_h[tm - N_META:]
    vt_meta = vt_h[:, tm - N_META:]

    attn_main = pl.pallas_call(
        _attn_main_kernel,
        grid=(bsz, nq),
        in_specs=[
            pl.BlockSpec((tq, HEADS * LANES), lambda b, qi: (b * nq + qi, 0)),
            pl.BlockSpec((seq, HEADS * LANES), lambda b, qi: (b, 0)),
            pl.BlockSpec((ATTN_W, seq), lambda b, qi: (0, b)),
            pl.BlockSpec(k_meta.shape, lambda b, qi: (0, 0)),
            pl.BlockSpec(vt_meta.shape, lambda b, qi: (0, 0)),
        ],
        out_specs=pl.BlockSpec((tq, ATTN_W), lambda b, qi: (b * nq + qi, 0)),
        out_shape=jax.ShapeDtypeStruct((bsz * seq, ATTN_W), BF16),
        scratch_shapes=_attn_scratch(tq) + [pltpu.VMEM((HEADS, tq, tq), F32),
                                            pltpu.VMEM((HEADS, tq, tq), F32),
                                            pltpu.VMEM((HEADS, tq), F32),
                                            pltpu.VMEM((HEADS, tq), F32),
                                            pltpu.VMEM((HEADS, N_META, tq), F32)],
        compiler_params=_params(2),
        name="attn_main",
    )(q_m, k_m, vt_m, k_meta, vt_meta)

    attn_head = pl.pallas_call(
        _attn_head_kernel,
        grid=(1,),
        in_specs=[
            pl.BlockSpec((META_BLK, HEADS * LANES), lambda i: (hm - 1, 0)),
            pl.BlockSpec(k_meta.shape, lambda i: (0, 0)),
            pl.BlockSpec(vt_meta.shape, lambda i: (0, 0)),
        ],
        out_specs=pl.BlockSpec((META_BLK, ATTN_W), lambda i: (0, 0)),
        out_shape=jax.ShapeDtypeStruct((META_BLK, ATTN_W), BF16),
        scratch_shapes=_attn_scratch(META_BLK),
        compiler_params=_params(1),
        name="attn_head",
    )(q_h, k_meta, vt_meta)
    return attn_main, attn_head


def _s5_prep_kernel(lre_ref, lim_ref, ldt_ref, bre_ref, bim_ref, lbr_ref, lbi_ref, bbr_ref, bbi_ref):
    lre = lre_ref[...]
    lim = lim_ref[...]
    dt = jnp.exp(ldt_ref[...])
    mag = jnp.exp(lre * dt)
    ang = lim * dt
    lbr = mag * jnp.cos(ang)
    lbi = mag * jnp.sin(ang)
    lbr_ref[...] = lbr
    lbi_ref[...] = lbi
    nr = lbr - 1.0
    ni = lbi
    den = lre * lre + lim * lim
    cr = (nr * lre + ni * lim) / den
    ci = (ni * lre - nr * lim) / den
    bre = bre_ref[...]
    bim = bim_ref[...]
    bbr_ref[...] = cr * bre - ci * bim
    bbi_ref[...] = cr * bim + ci * bre


def _s5_prep(lam_re, lam_im, log_dt, b_re, b_im):
    n = SSM_GROUPS * SSM_STATE
    col = lambda a: a.reshape(n, 1).astype(F32)
    ldt = jnp.broadcast_to(log_dt.astype(F32)[:, None], (SSM_GROUPS, SSM_STATE))
    return pl.pallas_call(
        _s5_prep_kernel,
        out_shape=[jax.ShapeDtypeStruct((n, 1), F32), jax.ShapeDtypeStruct((n, 1), F32),
                   jax.ShapeDtypeStruct((n, SSM_GROUP), F32), jax.ShapeDtypeStruct((n, SSM_GROUP), F32)],
        name="s5_prep",
    )(col(lam_re), col(lam_im), col(ldt), b_re.reshape(n, SSM_GROUP).astype(F32),
      b_im.reshape(n, SSM_GROUP).astype(F32))


def _gelu_tanh(x):
    c = math.sqrt(2.0 / math.pi)
    return 0.5 * x * (1.0 + jnp.tanh(c * (x + 0.044715 * (x * x * x))))


def _s5_kernel(u_ref, bblk_ref, cblk_ref, lamr_ref, lami_ref, d_ref, init_ref, z_ref, fin_ref,
               utb_sc, st_sc, ztb_sc, carry_sc, *, nb, tc, nch):
    c = pl.program_id(0)
    half_re = HALF_STATE // 2
    lane_chunk = 256
    n_lt = SSM_W // LANES

    @pl.when(c == 0)
    def _():
        carry_sc[...] = init_ref[...]

    def scan_lanes(ch, h, j, ub):
        lam_lo = h * half_re + j * lane_chunk
        lr = jnp.broadcast_to(lamr_ref[:, lam_lo:lam_lo + lane_chunk], (nb, lane_chunk))
        li = jnp.broadcast_to(lami_ref[:, lam_lo:lam_lo + lane_chunk], (nb, lane_chunk))
        re_l = slice(j * lane_chunk, (j + 1) * lane_chunk)
        im_l = slice(re_l.start + half_re, re_l.stop + half_re)
        re = slice(h * HALF_STATE + re_l.start, h * HALF_STATE + re_l.stop)
        im = slice(h * HALF_STATE + im_l.start, h * HALF_STATE + im_l.stop)
        bre = _dot_nt(ub, bblk_ref[h, re_l, :])
        bim = _dot_nt(ub, bblk_ref[h, im_l, :])
        sre, sim = carry_sc[:, re], carry_sc[:, im]
        for t in range(tc):
            rows = slice(t * nb, (t + 1) * nb)
            nre = lr * sre - li * sim + bre[rows]
            nim = lr * sim + li * sre + bim[rows]
            st_sc[ch, rows, re] = nre.astype(BF16)
            st_sc[ch, rows, im] = nim.astype(BF16)
            sre, sim = nre, nim
        carry_sc[:, re] = sre
        carry_sc[:, im] = sim

    for ch in range(nch):
        for b in range(nb):
            for j in range(n_lt):
                lo = b * SSM_W + j * LANES
                utb_sc[ch * n_lt + j, pl.ds(b, tc, stride=nb), :] = (
                    u_ref[ch * tc:(ch + 1) * tc, lo:lo + LANES].astype(F32))
        for h in range(2):
            ub = jnp.concatenate([utb_sc[ch * n_lt + 2 * h], utb_sc[ch * n_lt + 2 * h + 1]],
                                 axis=1).astype(BF16)
            for j in range(half_re // lane_chunk):
                scan_lanes(ch, h, j, ub)
        ys = [_dot_nt(st_sc[ch, :, h * HALF_STATE:(h + 1) * HALF_STATE], cblk_ref[h]) for h in range(2)]
        for j in range(n_lt):
            y = (ys[j // 2][:, (j % 2) * LANES:(j % 2 + 1) * LANES]
                 + d_ref[:, j * LANES:(j + 1) * LANES] * utb_sc[ch * n_lt + j])
            ztb_sc[ch * n_lt + j] = _gelu_tanh(y)
        for b in range(nb):
            for j in range(n_lt):
                lo = b * SSM_W + j * LANES
                z_ref[ch * tc:(ch + 1) * tc, lo:lo + LANES] = (
                    ztb_sc[ch * n_lt + j, pl.ds(b, tc, stride=nb), :].astype(z_ref.dtype))

    fin_ref[...] = carry_sc[...]


def _s5(u_tm, bblk, cblk, lamr, lami, dskip, init, *, nb, tc, nch):
    steps, width = u_tm.shape
    blk = tc * nch
    n_lt = SSM_W // LANES
    assert width == nb * SSM_W and steps % blk == 0 and init.shape == (nb, STATE_W)
    return pl.pallas_call(
        functools.partial(_s5_kernel, nb=nb, tc=tc, nch=nch),
        grid=(steps // blk,),
        in_specs=[
            pl.BlockSpec((blk, width), lambda c: (c, 0)),
            _const_spec(bblk.shape),
            _const_spec(cblk.shape),
            _const_spec(lamr.shape),
            _const_spec(lami.shape),
            _const_spec(dskip.shape),
            _const_spec(init.shape),
        ],
        out_specs=[pl.BlockSpec((blk, width), lambda c: (c, 0)),
                   pl.BlockSpec((nb, STATE_W), lambda c: (0, 0))],
        out_shape=[jax.ShapeDtypeStruct((steps, width), BF16),
                   jax.ShapeDtypeStruct((nb, STATE_W), F32)],
        scratch_shapes=[pltpu.VMEM((nch * n_lt, tc * nb, LANES), F32),
                        pltpu.VMEM((nch, tc * nb, STATE_W), BF16),
                        pltpu.VMEM((nch * n_lt, tc * nb, LANES), F32),
                        pltpu.VMEM((nb, STATE_W), F32)],
        compiler_params=_params(1),
        name="s5_scan",
    )(u_tm, bblk, cblk, lamr, lami, dskip, init)


def _s5_block_weights(bbr, bbi, c_re, c_im):
    eye = jnp.eye(HALF_GROUPS, dtype=F32)
    bb = jnp.stack([bbr, bbi]).reshape(2, 2, HALF_GROUPS, SSM_STATE, 1, SSM_GROUP)
    bblk_t = jnp.swapaxes(bb * eye[None, None, :, None, :, None], 0, 1)
    cc = jnp.stack([c_re.astype(F32), -c_im.astype(F32)])
    cc = cc.reshape(2, 2, HALF_GROUPS, SSM_GROUP, 1, SSM_STATE)
    cblk_t = jnp.transpose(cc * eye[None, None, :, None, :, None], (1, 2, 3, 0, 4, 5))
    return (bblk_t.reshape(2, HALF_STATE, HALF_U).astype(BF16),
            cblk_t.reshape(2, HALF_U, HALF_STATE).astype(BF16))


def _mixffn_kernel(attn_ref, z_ref, gate_ref, x_ref, wao_ref, wglu_ref, wo_ref, g2_ref,
                   wup_ref, cw_ref, cb_ref, wdn_ref, g3_ref, tin_ref,
                   out_ref, tout_ref, carry_sc, act_sc, h1_sc, *, nt, zero_rows):
    i = pl.program_id(0)

    @pl.when(i % nt == 0)
    def _():
        carry_sc[...] = tin_ref[...]

    ya = _dot(attn_ref[...], wao_ref[...])
    zg = _dot(z_ref[...], wglu_ref[...])
    ys = zg[:, :D_MODEL] * jax.nn.sigmoid(zg[:, D_MODEL:])
    g = gate_ref[...].astype(F32)
    merged = g[:, :D_MODEL] * ya + g[:, D_MODEL:] * ys
    h1 = x_ref[...] + _dot(merged.astype(BF16), wo_ref[...])
    tm = h1.shape[0]
    if zero_rows:
        r = lax.broadcasted_iota(jnp.int32, (tm, 1), 0)
        h1 = jnp.where(r < zero_rows, 0.0, h1)
    h1_sc[...] = h1
    xn = _rms(h1, g2_ref[...]).astype(BF16)

    row = lax.broadcasted_iota(jnp.int32, (SUBLANES, 1), 0)
    for c in range(D_FF // FF_CHUNK):
        conv = []
        for part in range(2):
            lo = part * D_FF + c * FF_CHUNK
            cols = slice(lo, lo + FF_CHUNK)
            h = _dot(xn, wup_ref[:, cols])
            prev = carry_sc[:, cols]
            p1 = prev[SUBLANES - 1:SUBLANES, :]
            p2 = prev[SUBLANES - 2:SUBLANES - 1, :]
            r1 = pltpu.roll(h, 1, 0)
            r2 = pltpu.roll(h, 2, 0)
            t1 = jnp.where(row == 0, p1, r1[:SUBLANES])
            t2 = jnp.where(row == 0, p2, jnp.where(row == 1, p1, r2[:SUBLANES]))
            hm1 = jnp.concatenate([t1, r1[SUBLANES:]], axis=0)
            hm2 = jnp.concatenate([t2, r2[SUBLANES:]], axis=0)
            w = cw_ref[:, cols]
            conv.append(cb_ref[:, cols] + (w[0:1, :] * hm2 + w[1:2, :] * hm1 + w[2:3, :] * h))
            carry_sc[:, cols] = h[tm - SUBLANES:tm, :]
        val, gate = conv
        act_sc[:, c * FF_CHUNK:(c + 1) * FF_CHUNK] = ((gate * jax.nn.sigmoid(gate)) * val).astype(BF16)

    tout_ref[...] = carry_sc[...]
    out_ref[...] = _rms(h1_sc[...] + _dot(act_sc[...], wdn_ref[...]), g3_ref[...])


def _mixffn(attn, z_tm, gate, x2d, tail_in, wao, wglu, wo, g2, wup, cw, cb, wdn, g3, *,
            bsz, seq, tm, zero_rows=0):
    nt = seq // tm
    return pl.pallas_call(
        functools.partial(_mixffn_kernel, nt=nt, zero_rows=zero_rows),
        grid=(bsz * nt,),
        in_specs=[
            pl.BlockSpec((tm, ATTN_W), lambda i: (i, 0)),
            pl.BlockSpec((tm, SSM_W), lambda i: (i % nt, i // nt)),
            pl.BlockSpec((tm, 2 * D_MODEL), lambda i: (i, 0)),
            pl.BlockSpec((tm, D_MODEL), lambda i: (i, 0)),
            _const_spec(wao.shape),
            _const_spec(wglu.shape),
            _const_spec(wo.shape),
            _const_spec((1, D_MODEL)),
            _const_spec(wup.shape),
            _const_spec(cw.shape),
            _const_spec(cb.shape),
            _const_spec(wdn.shape),
            _const_spec((1, D_MODEL)),
            _const_spec(tail_in.shape),
        ],
        out_specs=[pl.BlockSpec((tm, D_MODEL), lambda i: (i, 0)),
                   pl.BlockSpec(tail_in.shape, lambda i: (0, 0))],
        out_shape=[jax.ShapeDtypeStruct((bsz * seq, D_MODEL), F32),
                   jax.ShapeDtypeStruct(tail_in.shape, F32)],
        scratch_shapes=[pltpu.VMEM(tail_in.shape, F32), pltpu.VMEM((tm, D_FF), BF16),
                        pltpu.VMEM((tm, D_MODEL), F32)],
        compiler_params=_params(1),
        name="mixffn",
    )(attn, z_tm, gate, x2d, wao, wglu, wo, g2, wup, cw, cb, wdn, g3, tail_in)


def _forward(x, meta_tokens, norm_mix_g, w_in, b_forget, w_attn_out, lam_re, lam_im, b_re, b_im,
             c_re, c_im, d_skip, log_dt, w_glu, w_o, norm_ffn_g, w_up, conv_w, conv_b, w_down,
             norm_final_g, *, tm, tq, tc):
    bsz, seq, _ = x.shape
    assert seq % tm == 0 and seq % tq == 0
    assert seq % (tc * S5_CHUNKS_PER_STEP) == 0
    th = META_BLK
    assert th >= tc >= N_META
    x2d = x.reshape(bsz * seq, D_MODEL)
    head = jnp.concatenate([jnp.zeros((th - N_META, D_MODEL), F32), meta_tokens.astype(F32)], axis=0)

    q_end, k_end, v_end = ATTN_W, 2 * ATTN_W, 3 * ATTN_W
    f_end = v_end + HEADS
    u_end = f_end + SSM_W
    wq = w_in[:, :q_end].astype(BF16)
    wk = w_in[:, q_end:k_end].astype(BF16)
    wvt = jnp.transpose(w_in[:, k_end:v_end]).astype(BF16)
    place, ones = _forget_lane_weights()
    wf32 = w_in[:, v_end:f_end]
    wf_hi = wf32.astype(BF16)
    wf_lo = (wf32 - wf_hi.astype(F32)).astype(BF16)
    wf = jnp.pad(jnp.concatenate([wf_hi, wf_lo], axis=1), ((0, 0), (0, LANES - 2 * HEADS)))
    bfp = jnp.pad(b_forget.astype(F32), (0, LANES - HEADS)).reshape(1, LANES)
    wu = w_in[:, f_end:u_end].astype(BF16)
    wg = w_in[:, u_end:].astype(BF16)
    tri = lambda n: jnp.asarray(np.tril(np.ones((n, n), np.float32)), BF16)
    g1 = norm_mix_g.reshape(1, D_MODEL)
    inproj = functools.partial(_inproj, g1=g1, wq=wq, wk=wk, wvt=wvt, wu=wu, wg=wg, wf=wf,
                               bfp=bfp, place=place, ones=ones)
    q_h, k_h, vt_h, u_h, gate_h, f_meta = inproj(head, jnp.zeros((SUBLANES, LANES), F32), tri=tri(th),
                                                 bsz=1, seq=th, tm=th)
    q_m, k_m, vt_m, u_m, gate_m, _ = inproj(x2d, f_meta, tri=tri(tm), bsz=bsz, seq=seq, tm=tm)

    attn_m, attn_h = _attention(q_m, k_m, vt_m, q_h, k_h, vt_h, bsz=bsz, seq=seq, tm=th, tq=tq)

    lbr, lbi, bbr, bbi = _s5_prep(lam_re, lam_im, log_dt, b_re, b_im)
    bblk, cblk = _s5_block_weights(bbr.reshape(SSM_GROUPS, SSM_STATE, SSM_GROUP),
                                   bbi.reshape(SSM_GROUPS, SSM_STATE, SSM_GROUP), c_re, c_im)
    lamr = lbr.reshape(1, SSM_GROUPS * SSM_STATE)
    lami = lbi.reshape(1, SSM_GROUPS * SSM_STATE)
    dsk = d_skip.astype(F32).reshape(1, SSM_W)
    u_meta = jnp.pad(u_h[th - tc:], ((0, 0), (0, (S5_META_BATCH - 1) * SSM_W)))
    z_meta, st_meta = _s5(u_meta, bblk, cblk, lamr, lami, dsk,
                          jnp.zeros((S5_META_BATCH, STATE_W), F32), nb=S5_META_BATCH, tc=tc, nch=1)
    z_h = jnp.concatenate([jnp.zeros((th - tc, SSM_W), BF16), z_meta[:, :SSM_W]], axis=0)
    z_m, _ = _s5(u_m, bblk, cblk, lamr, lami, dsk, jnp.broadcast_to(st_meta[0:1], (bsz, STATE_W)),
                 nb=bsz, tc=tc, nch=S5_CHUNKS_PER_STEP)

    cw = jnp.pad(conv_w.astype(F32), ((0, SUBLANES - CONV_W), (0, 0)))
    mixffn = functools.partial(
        _mixffn, wao=w_attn_out.astype(BF16), wglu=w_glu.astype(BF16), wo=w_o.astype(BF16),
        g2=norm_ffn_g.reshape(1, D_MODEL), wup=w_up.astype(BF16), cw=cw,
        cb=conv_b.astype(F32).reshape(1, 2 * D_FF), wdn=w_down.astype(BF16),
        g3=norm_final_g.reshape(1, D_MODEL))
    _, tail = mixffn(attn_h, z_h, gate_h, head, jnp.zeros((SUBLANES, 2 * D_FF), F32),
                     bsz=1, seq=th, tm=th, zero_rows=th - N_META)
    out, _ = mixffn(attn_m, z_m, gate_m, x2d, tail, bsz=bsz, seq=seq, tm=tm)
    return out.reshape(bsz, seq, D_MODEL)


def kernel(x, meta_tokens, norm_mix_g, w_in, b_forget, w_attn_out, ssm_lambda_re, ssm_lambda_im,
           ssm_b_re, ssm_b_im, ssm_c_re, ssm_c_im, ssm_d, ssm_log_dt, w_glu, w_o, norm_ffn_g,
           w_ffn_up, ffn_conv_w, ffn_conv_b, w_ffn_down, norm_final_g):
    assert norm_mix_g.shape[0] == 1, "single-layer block"
    return _forward(x, meta_tokens, norm_mix_g[0], w_in[0], b_forget[0], w_attn_out[0],
                    ssm_lambda_re[0], ssm_lambda_im[0], ssm_b_re[0], ssm_b_im[0], ssm_c_re[0],
                    ssm_c_im[0], ssm_d[0], ssm_log_dt[0], w_glu[0], w_o[0], norm_ffn_g[0],
                    w_ffn_up[0], ffn_conv_w[0], ffn_conv_b[0], w_ffn_down[0], norm_final_g,
                    tm=TM, tq=TQ, tc=S5_CHUNK)
```

```python
import functools
import math

import jax
import jax.numpy as jnp
import numpy as np
from jax import lax
from jax.experimental import pallas as pl
from jax.experimental.pallas import tpu as pltpu

F32 = jnp.float32
BF16 = jnp.bfloat16

D_MODEL = 1024
N_META = 16
HEADS = 8
HEAD_DIM = 64
ATTN_W = HEADS * HEAD_DIM
SSM_W = 512
SSM_GROUP = 16
SSM_GROUPS = SSM_W // SSM_GROUP
SSM_STATE = 64
STATE_W = 2 * SSM_GROUPS * SSM_STATE
HALF_GROUPS = SSM_GROUPS // 2
HALF_U = HALF_GROUPS * SSM_GROUP
HALF_STATE = STATE_W // 2
D_FF = 2816
CONV_W = 3
RMS_EPS = 1e-6

LANES = 128
SUBLANES = 8
BF16_ROWS = 16
PAIR_W = 2 * HEAD_DIM
N_PAIRS = HEADS // 2
ACC_ROWS = HEAD_DIM + BF16_ROWS
META_BLK = 128
NEG = -1e30
LOG2E = math.log2(math.e)

TM = 512
TQ = 256
S5_CHUNK = 32
S5_CHUNKS_PER_STEP = 2
S5_META_BATCH = 16
FF_CHUNK = 256
VMEM_LIMIT = 56 * 1024 * 1024


def _const_spec(shape):
    nd = len(shape)
    return pl.BlockSpec(shape, lambda *_: (0,) * nd, pipeline_mode=pl.Buffered(1))


def _params(n_grid):
    return pltpu.CompilerParams(dimension_semantics=("arbitrary",) * n_grid,
                                vmem_limit_bytes=VMEM_LIMIT)


def _rms(x, g):
    ms = jnp.mean(x * x, axis=-1, keepdims=True)
    return x * lax.rsqrt(ms + RMS_EPS) * g


def _dot(a, b):
    return jnp.dot(a, b, preferred_element_type=F32)


def _dot_nt(a, b):
    return lax.dot_general(a, b, (((1,), (1,)), ((), ())), preferred_element_type=F32)


def _pieces3(v):
    a = v.astype(BF16).astype(F32)
    r = v - a
    b = r.astype(BF16).astype(F32)
    c = (r - b).astype(BF16).astype(F32)
    return a, b, c


def _pack_pieces(v):
    a, b, c = _pieces3(v)
    return (a + pltpu.roll(b, HEADS, 1) + pltpu.roll(c, 2 * HEADS, 1)).astype(BF16)


def _unpack_sum(p):
    return p + pltpu.roll(p, LANES - HEADS, 1) + pltpu.roll(p, LANES - 2 * HEADS, 1)


def _inproj_kernel(x_ref, g_ref, wq_ref, wk_ref, wvt_ref, wu_ref, wg_ref, wf_ref, bf_ref,
                   tri_ref, place_ref, ones_ref, cin_ref,
                   q_ref, k_ref, vt_ref, u_ref, gate_ref, cout_ref, carry_sc, *, nt):
    i = pl.program_id(0)

    @pl.when(i % nt == 0)
    def _():
        carry_sc[...] = cin_ref[...]

    xh = _rms(x_ref[...], g_ref[...]).astype(BF16)
    tm = xh.shape[0]

    lane = lax.broadcasted_iota(jnp.int32, (tm, LANES), 1)
    fr = _dot(xh, wf_ref[...])
    vt_ref[...] = _dot_nt(wvt_ref[...], xh).astype(BF16)
    u_ref[...] = _dot(xh, wu_ref[...]).astype(BF16)
    f = fr + pltpu.roll(fr, LANES - HEADS, 1) + bf_ref[...]
    logf = jnp.where(lane < HEADS, jnp.minimum(f, 0.0) - jnp.log1p(jnp.exp(-jnp.abs(f))), 0.0)
    packed = _pack_pieces(logf)
    gate_ref[...] = jax.nn.sigmoid(_dot(xh, wg_ref[...])).astype(BF16)
    cum = _unpack_sum(_dot(tri_ref[...], packed)) + carry_sc[0:1, :]
    cum = jnp.where(lane < HEADS, cum, 0.0)
    carry_sc[...] = jnp.broadcast_to(cum[tm - 1:tm, :], carry_sc.shape)
    cout_ref[...] = carry_sc[...]

    qr = _dot(xh, wq_ref[...]) * (LOG2E / math.sqrt(HEAD_DIM))
    kr = _dot(xh, wk_ref[...])
    aug = _dot(_pack_pieces(cum * LOG2E), place_ref[...]) + ones_ref[...]
    for res, out_ref, aug_lo in ((qr, q_ref, 0), (kr, k_ref, HEADS * LANES)):
        for h in range(HEADS):
            src = res[:, (h // 2) * LANES:(h // 2 + 1) * LANES]
            if h % 2:
                src = pltpu.roll(src, HEAD_DIM, 1)
            blk = jnp.where(lane < HEAD_DIM, src, aug[:, aug_lo + h * LANES:aug_lo + (h + 1) * LANES])
            out_ref[:, h * LANES:(h + 1) * LANES] = blk.astype(BF16)


def _inproj(x2d, carry_in, g1, wq, wk, wvt, wu, wg, wf, bfp, tri, place, ones, *, bsz, seq, tm):
    nt = seq // tm
    n_tiles = bsz * nt
    rows = bsz * seq
    in_specs = [
        pl.BlockSpec((tm, D_MODEL), lambda i: (i, 0)),
        _const_spec((1, D_MODEL)),
        _const_spec(wq.shape),
        _const_spec(wk.shape),
        _const_spec(wvt.shape),
        _const_spec(wu.shape),
        _const_spec(wg.shape),
        _const_spec(wf.shape),
        _const_spec((1, LANES)),
        _const_spec((tm, tm)),
        _const_spec(place.shape),
        _const_spec(ones.shape),
        _const_spec((SUBLANES, LANES)),
    ]
    out_specs = [
        pl.BlockSpec((tm, HEADS * LANES), lambda i: (i, 0)),
        pl.BlockSpec((tm, HEADS * LANES), lambda i: (i, 0)),
        pl.BlockSpec((ATTN_W, tm), lambda i: (0, i)),
        pl.BlockSpec((tm, SSM_W), lambda i: (i % nt, i // nt)),
        pl.BlockSpec((tm, 2 * D_MODEL), lambda i: (i, 0)),
        pl.BlockSpec((SUBLANES, LANES), lambda i: (0, 0)),
    ]
    out_shape = [
        jax.ShapeDtypeStruct((rows, HEADS * LANES), BF16),
        jax.ShapeDtypeStruct((rows, HEADS * LANES), BF16),
        jax.ShapeDtypeStruct((ATTN_W, rows), BF16),
        jax.ShapeDtypeStruct((seq, bsz * SSM_W), BF16),
        jax.ShapeDtypeStruct((rows, 2 * D_MODEL), BF16),
        jax.ShapeDtypeStruct((SUBLANES, LANES), F32),
    ]
    return pl.pallas_call(
        functools.partial(_inproj_kernel, nt=nt),
        grid=(n_tiles,),
        in_specs=in_specs,
        out_specs=out_specs,
        out_shape=out_shape,
        scratch_shapes=[pltpu.VMEM((SUBLANES, LANES), F32)],
        compiler_params=_params(1),
        name="inproj",
    )(x2d, g1, wq, wk, wvt, wu, wg, wf, bfp, tri, place, ones, carry_in)


def _forget_lane_weights():
    place = np.zeros((LANES, 2 * HEADS * LANES), np.float32)
    ones = np.zeros((1, 2 * HEADS * LANES), np.float32)
    k_off = HEADS * LANES
    for piece in range(3):
        for h in range(HEADS):
            place[piece * HEADS + h, h * LANES + HEAD_DIM + piece] = 1.0
            place[piece * HEADS + h, k_off + h * LANES + HEAD_DIM + 3 + piece] = -1.0
            ones[0, h * LANES + HEAD_DIM + 3 + piece] = 1.0
            ones[0, k_off + h * LANES + HEAD_DIM + piece] = 1.0
    return jnp.asarray(place, BF16), jnp.asarray(ones)


def _attn_scores(k, q):
    return _dot_nt(k, q)


def _attn_update(h, blocks, m_sc, acc_sc, smax=None):
    rows = slice(h * ACC_ROWS, (h + 1) * ACC_ROWS)
    m_old = m_sc[h:h + 1, :]
    if smax is None:
        m_new = m_old
        for s, _ in blocks:
            m_new = jnp.maximum(m_new, jnp.max(s, axis=0, keepdims=True))
    else:
        m_new = jnp.maximum(m_old, smax)
    alpha = jnp.exp2(m_old - m_new)
    pv = None
    for s, vt in blocks:
        p = jnp.exp2(s - m_new).astype(BF16)
        lhs = jnp.concatenate([vt, jnp.ones((BF16_ROWS, vt.shape[1]), BF16)], axis=0)
        d = _dot(lhs, p)
        pv = d if pv is None else pv + d
    acc_sc[rows, :] = alpha * acc_sc[rows, :] + pv
    m_sc[h:h + 1, :] = m_new


def _attn_init(m_sc, acc_sc):
    m_sc[...] = jnp.full_like(m_sc, NEG)
    acc_sc[...] = jnp.zeros_like(acc_sc)


def _attn_finish(o_ref, rows, acc_sc):
    for pair in range(N_PAIRS):
        halves = []
        for h in (2 * pair, 2 * pair + 1):
            base = h * ACC_ROWS
            inv = 1.0 / acc_sc[base + HEAD_DIM:base + HEAD_DIM + 1, :]
            halves.append(acc_sc[base:base + HEAD_DIM, :] * inv)
        ot = jnp.concatenate(halves, axis=0)
        o_ref[rows, pair * PAIR_W:(pair + 1) * PAIR_W] = jnp.transpose(ot).astype(o_ref.dtype)


def _attn_main_kernel(q_ref, k_ref, vt_ref, km_ref, vtm_ref, o_ref, m_sc, acc_sc,
                      s0_sc, s1_sc, c0_sc, c1_sc, sm_sc, *, tq):
    def query_tile(qi, carry):
        qrows = pl.ds(pl.multiple_of(qi * tq, tq), tq)
        _attn_init(m_sc, acc_sc)

        def scores(kv, s_sc, c_sc, heads=range(HEADS)):
            off = pl.multiple_of(kv * tq, tq)
            for h in heads:
                lanes = slice(h * LANES, (h + 1) * LANES)
                s = _attn_scores(k_ref[pl.ds(off, tq), lanes], q_ref[qrows, lanes])
                s_sc[h] = s
                c_sc[h:h + 1, :] = jnp.max(s, axis=0, keepdims=True)

        def update(kv, s_sc, c_sc, diagonal, heads=range(HEADS)):
            off = pl.multiple_of(kv * tq, tq)
            if diagonal:
                key = lax.broadcasted_iota(jnp.int32, (tq, tq), 0)
                qry = lax.broadcasted_iota(jnp.int32, (tq, tq), 1)
            for h in heads:
                vt = vt_ref[h * HEAD_DIM:(h + 1) * HEAD_DIM, pl.ds(off, tq)]
                if diagonal:
                    s = jnp.where(key <= qry, s_sc[h], NEG)
                    _attn_update(h, [(s, vt), (sm_sc[h], vtm_ref[h * HEAD_DIM:(h + 1) * HEAD_DIM, :])],
                                 m_sc, acc_sc)
                else:
                    _attn_update(h, [(s_sc[h], vt)], m_sc, acc_sc, smax=c_sc[h:h + 1, :])

        for h in range(HEADS):
            lanes = slice(h * LANES, (h + 1) * LANES)
            s = _attn_scores(jnp.concatenate([k_ref[0:tq, lanes], km_ref[:, lanes]], axis=0),
                             q_ref[qrows, lanes])
            s0_sc[h] = s[:tq]
            c0_sc[h:h + 1, :] = jnp.max(s[:tq], axis=0, keepdims=True)
            sm_sc[h] = s[tq:]

        def overlapped(kv_next, s_next, c_next, kv, s_cur, c_cur):
            for h in range(HEADS):
                scores(kv_next, s_next, c_next, heads=(h,))
                update(kv, s_cur, c_cur, False, heads=(h,))

        def body(jj, c2):
            j = 2 * jj
            overlapped(j + 1, s1_sc, c1_sc, j, s0_sc, c0_sc)
            overlapped(j + 2, s0_sc, c0_sc, j + 1, s1_sc, c1_sc)
            return c2

        lax.fori_loop(0, qi // 2, body, 0)

        @pl.when(qi % 2 == 0)
        def _():
            update(qi, s0_sc, c0_sc, True)

        @pl.when(qi % 2 == 1)
        def _():
            overlapped(qi, s1_sc, c1_sc, qi - 1, s0_sc, c0_sc)
            update(qi, s1_sc, c1_sc, True)

        _attn_finish(o_ref, qrows, acc_sc)
        return carry

    lax.fori_loop(0, q_ref.shape[0] // tq, query_tile, 0)


def _attn_head_kernel(q_ref, km_ref, vtm_ref, o_ref, m_sc, acc_sc):
    tq = q_ref.shape[0]
    _attn_init(m_sc, acc_sc)
    key = lax.broadcasted_iota(jnp.int32, (N_META, tq), 0)
    qry = lax.broadcasted_iota(jnp.int32, (N_META, tq), 1)
    mask = key <= qry - (tq - N_META)
    for h in range(HEADS):
        lanes = slice(h * LANES, (h + 1) * LANES)
        s = jnp.where(mask, _attn_scores(km_ref[:, lanes], q_ref[:, lanes]), NEG)
        _attn_update(h, [(s, vtm_ref[h * HEAD_DIM:(h + 1) * HEAD_DIM, :])], m_sc, acc_sc)
    _attn_finish(o_ref, slice(None), acc_sc)


def _attn_scratch(tq):
    return [pltpu.VMEM((HEADS, tq), F32), pltpu.VMEM((HEADS * ACC_ROWS, tq), F32)]


def _attention(q_m, k_m, vt_m, q_h, k_h, vt_h, *, bsz, seq, tm, tq):
    hm = tm // META_BLK
    k_meta = k_h[tm - N_META:]
    vt_meta = vt_h[:, tm - N_META:]

    attn_main = pl.pallas_call(
        functools.partial(_attn_main_kernel, tq=tq),
        grid=(bsz,),
        in_specs=[
            pl.BlockSpec((seq, HEADS * LANES), lambda b: (b, 0)),
            pl.BlockSpec((seq, HEADS * LANES), lambda b: (b, 0)),
            pl.BlockSpec((ATTN_W, seq), lambda b: (0, b)),
            pl.BlockSpec(k_meta.shape, lambda b: (0, 0)),
            pl.BlockSpec(vt_meta.shape, lambda b: (0, 0)),
        ],
        out_specs=pl.BlockSpec((seq, ATTN_W), lambda b: (b, 0)),
        out_shape=jax.ShapeDtypeStruct((bsz * seq, ATTN_W), BF16),
        scratch_shapes=_attn_scratch(tq) + [pltpu.VMEM((HEADS, tq, tq), F32),
                                            pltpu.VMEM((HEADS, tq, tq), F32),
                                            pltpu.VMEM((HEADS, tq), F32),
                                            pltpu.VMEM((HEADS, tq), F32),
                                            pltpu.VMEM((HEADS, N_META, tq), F32)],
        compiler_params=_params(1),
        name="attn_main",
    )(q_m, k_m, vt_m, k_meta, vt_meta)

    attn_head = pl.pallas_call(
        _attn_head_kernel,
        grid=(1,),
        in_specs=[
            pl.BlockSpec((META_BLK, HEADS * LANES), lambda i: (hm - 1, 0)),
            pl.BlockSpec(k_meta.shape, lambda i: (0, 0)),
            pl.BlockSpec(vt_meta.shape, lambda i: (0, 0)),
        ],
        out_specs=pl.BlockSpec((META_BLK, ATTN_W), lambda i: (0, 0)),
        out_shape=jax.ShapeDtypeStruct((META_BLK, ATTN_W), BF16),
        scratch_shapes=_attn_scratch(META_BLK),
        compiler_params=_params(1),
        name="attn_head",
    )(q_h, k_meta, vt_meta)
    return attn_main, attn_head


def _s5_prep_kernel(lre_ref, lim_ref, ldt_ref, bre_ref, bim_ref, lbr_ref, lbi_ref, bbr_ref, bbi_ref):
    lre = lre_ref[...]
    lim = lim_ref[...]
    dt = jnp.exp(ldt_ref[...])
    mag = jnp.exp(lre * dt)
    ang = lim * dt
    lbr = mag * jnp.cos(ang)
    lbi = mag * jnp.sin(ang)
    lbr_ref[...] = lbr
    lbi_ref[...] = lbi
    nr = lbr - 1.0
    ni = lbi
    den = lre * lre + lim * lim
    cr = (nr * lre + ni * lim) / den
    ci = (ni * lre - nr * lim) / den
    bre = bre_ref[...]
    bim = bim_ref[...]
    bbr_ref[...] = cr * bre - ci * bim
    bbi_ref[...] = cr * bim + ci * bre


def _s5_prep(lam_re, lam_im, log_dt, b_re, b_im):
    n = SSM_GROUPS * SSM_STATE
    col = lambda a: a.reshape(n, 1).astype(F32)
    ldt = jnp.broadcast_to(log_dt.astype(F32)[:, None], (SSM_GROUPS, SSM_STATE))
    return pl.pallas_call(
        _s5_prep_kernel,
        out_shape=[jax.ShapeDtypeStruct((n, 1), F32), jax.ShapeDtypeStruct((n, 1), F32),
                   jax.ShapeDtypeStruct((n, SSM_GROUP), F32), jax.ShapeDtypeStruct((n, SSM_GROUP), F32)],
        name="s5_prep",
    )(col(lam_re), col(lam_im), col(ldt), b_re.reshape(n, SSM_GROUP).astype(F32),
      b_im.reshape(n, SSM_GROUP).astype(F32))


def _gelu_tanh(x):
    c = math.sqrt(2.0 / math.pi)
    return 0.5 * x * (1.0 + jnp.tanh(c * (x + 0.044715 * (x * x * x))))


def _s5_kernel(u_ref, bblk_ref, cblk_ref, lamr_ref, lami_ref, d_ref, init_ref, z_ref, fin_ref,
               utb_sc, st_sc, ztb_sc, carry_sc, *, nb, tc, nch):
    c = pl.program_id(0)
    half_re = HALF_STATE // 2
    lane_chunk = 256
    n_lt = SSM_W // LANES

    @pl.when(c == 0)
    def _():
        carry_sc[...] = init_ref[...]

    def scan_lanes(ch, h, j, ub):
        lam_lo = h * half_re + j * lane_chunk
        lr = jnp.broadcast_to(lamr_ref[:, lam_lo:lam_lo + lane_chunk], (nb, lane_chunk))
        li = jnp.broadcast_to(lami_ref[:, lam_lo:lam_lo + lane_chunk], (nb, lane_chunk))
        re_l = slice(j * lane_chunk, (j + 1) * lane_chunk)
        im_l = slice(re_l.start + half_re, re_l.stop + half_re)
        re = slice(h * HALF_STATE + re_l.start, h * HALF_STATE + re_l.stop)
        im = slice(h * HALF_STATE + im_l.start, h * HALF_STATE + im_l.stop)
        bre = _dot_nt(ub, bblk_ref[h, re_l, :])
        bim = _dot_nt(ub, bblk_ref[h, im_l, :])
        sre, sim = carry_sc[:, re], carry_sc[:, im]
        for t in range(tc):
            rows = slice(t * nb, (t + 1) * nb)
            nre = lr * sre - li * sim + bre[rows]
            nim = lr * sim + li * sre + bim[rows]
            st_sc[ch, rows, re] = nre.astype(BF16)
            st_sc[ch, rows, im] = nim.astype(BF16)
            sre, sim = nre, nim
        carry_sc[:, re] = sre
        carry_sc[:, im] = sim

    for ch in range(nch):
        for b in range(nb):
            for j in range(n_lt):
                lo = b * SSM_W + j * LANES
                utb_sc[ch * n_lt + j, pl.ds(b, tc, stride=nb), :] = (
                    u_ref[ch * tc:(ch + 1) * tc, lo:lo + LANES].astype(F32))
        for h in range(2):
            ub = jnp.concatenate([utb_sc[ch * n_lt + 2 * h], utb_sc[ch * n_lt + 2 * h + 1]],
                                 axis=1).astype(BF16)
            for j in range(half_re // lane_chunk):
                scan_lanes(ch, h, j, ub)
        ys = [_dot_nt(st_sc[ch, :, h * HALF_STATE:(h + 1) * HALF_STATE], cblk_ref[h]) for h in range(2)]
        for j in range(n_lt):
            y = (ys[j // 2][:, (j % 2) * LANES:(j % 2 + 1) * LANES]
                 + d_ref[:, j * LANES:(j + 1) * LANES] * utb_sc[ch * n_lt + j])
            ztb_sc[ch * n_lt + j] = _gelu_tanh(y)
        for b in range(nb):
            for j in range(n_lt):
                lo = b * SSM_W + j * LANES
                z_ref[ch * tc:(ch + 1) * tc, lo:lo + LANES] = (
                    ztb_sc[ch * n_lt + j, pl.ds(b, tc, stride=nb), :].astype(z_ref.dtype))

    fin_ref[...] = carry_sc[...]


def _s5(u_tm, bblk, cblk, lamr, lami, dskip, init, *, nb, tc, nch):
    steps, width = u_tm.shape
    blk = tc * nch
    n_lt = SSM_W // LANES
    assert width == nb * SSM_W and steps % blk == 0 and init.shape == (nb, STATE_W)
    return pl.pallas_call(
        functools.partial(_s5_kernel, nb=nb, tc=tc, nch=nch),
        grid=(steps // blk,),
        in_specs=[
            pl.BlockSpec((blk, width), lambda c: (c, 0)),
            _const_spec(bblk.shape),
            _const_spec(cblk.shape),
            _const_spec(lamr.shape),
            _const_spec(lami.shape),
            _const_spec(dskip.shape),
            _const_spec(init.shape),
        ],
        out_specs=[pl.BlockSpec((blk, width), lambda c: (c, 0)),
                   pl.BlockSpec((nb, STATE_W), lambda c: (0, 0))],
        out_shape=[jax.ShapeDtypeStruct((steps, width), BF16),
                   jax.ShapeDtypeStruct((nb, STATE_W), F32)],
        scratch_shapes=[pltpu.VMEM((nch * n_lt, tc * nb, LANES), F32),
                        pltpu.VMEM((nch, tc * nb, STATE_W), BF16),
                        pltpu.VMEM((nch * n_lt, tc * nb, LANES), F32),
                        pltpu.VMEM((nb, STATE_W), F32)],
        compiler_params=_params(1),
        name="s5_scan",
    )(u_tm, bblk, cblk, lamr, lami, dskip, init)


def _s5_block_weights(bbr, bbi, c_re, c_im):
    eye = jnp.eye(HALF_GROUPS, dtype=F32)
    bb = jnp.stack([bbr, bbi]).reshape(2, 2, HALF_GROUPS, SSM_STATE, 1, SSM_GROUP)
    bblk_t = jnp.swapaxes(bb * eye[None, None, :, None, :, None], 0, 1)
    cc = jnp.stack([c_re.astype(F32), -c_im.astype(F32)])
    cc = cc.reshape(2, 2, HALF_GROUPS, SSM_GROUP, 1, SSM_STATE)
    cblk_t = jnp.transpose(cc * eye[None, None, :, None, :, None], (1, 2, 3, 0, 4, 5))
    return (bblk_t.reshape(2, HALF_STATE, HALF_U).astype(BF16),
            cblk_t.reshape(2, HALF_U, HALF_STATE).astype(BF16))


def _mixffn_kernel(attn_ref, z_ref, gate_ref, x_ref, wao_ref, wglu_ref, wo_ref, g2_ref,
                   wup_ref, cw_ref, cb_ref, wdn_ref, g3_ref, tin_ref,
                   out_ref, tout_ref, carry_sc, act_sc, h1_sc, *, nt, zero_rows):
    i = pl.program_id(0)

    @pl.when(i % nt == 0)
    def _():
        carry_sc[...] = tin_ref[...]

    ya = _dot(attn_ref[...], wao_ref[...])
    zg = _dot(z_ref[...], wglu_ref[...])
    ys = zg[:, :D_MODEL] * jax.nn.sigmoid(zg[:, D_MODEL:])
    g = gate_ref[...].astype(F32)
    merged = g[:, :D_MODEL] * ya + g[:, D_MODEL:] * ys
    h1 = x_ref[...] + _dot(merged.astype(BF16), wo_ref[...])
    tm = h1.shape[0]
    if zero_rows:
        r = lax.broadcasted_iota(jnp.int32, (tm, 1), 0)
        h1 = jnp.where(r < zero_rows, 0.0, h1)
    h1_sc[...] = h1
    xn = _rms(h1, g2_ref[...]).astype(BF16)

    row = lax.broadcasted_iota(jnp.int32, (SUBLANES, 1), 0)
    for c in range(D_FF // FF_CHUNK):
        conv = []
        for part in range(2):
            lo = part * D_FF + c * FF_CHUNK
            cols = slice(lo, lo + FF_CHUNK)
            h = _dot(xn, wup_ref[:, cols])
            prev = carry_sc[:, cols]
            p1 = prev[SUBLANES - 1:SUBLANES, :]
            p2 = prev[SUBLANES - 2:SUBLANES - 1, :]
            r1 = pltpu.roll(h, 1, 0)
            r2 = pltpu.roll(h, 2, 0)
            t1 = jnp.where(row == 0, p1, r1[:SUBLANES])
            t2 = jnp.where(row == 0, p2, jnp.where(row == 1, p1, r2[:SUBLANES]))
            hm1 = jnp.concatenate([t1, r1[SUBLANES:]], axis=0)
            hm2 = jnp.concatenate([t2, r2[SUBLANES:]], axis=0)
            w = cw_ref[:, cols]
            conv.append(cb_ref[:, cols] + (w[0:1, :] * hm2 + w[1:2, :] * hm1 + w[2:3, :] * h))
            carry_sc[:, cols] = h[tm - SUBLANES:tm, :]
        val, gate = conv
        act_sc[:, c * FF_CHUNK:(c + 1) * FF_CHUNK] = ((gate * jax.nn.sigmoid(gate)) * val).astype(BF16)

    tout_ref[...] = carry_sc[...]
    out_ref[...] = _rms(h1_sc[...] + _dot(act_sc[...], wdn_ref[...]), g3_ref[...])


def _mixffn(attn, z_tm, gate, x2d, tail_in, wao, wglu, wo, g2, wup, cw, cb, wdn, g3, *,
            bsz, seq, tm, zero_rows=0):
    nt = seq // tm
    return pl.pallas_call(
        functools.partial(_mixffn_kernel, nt=nt, zero_rows=zero_rows),
        grid=(bsz * nt,),
        in_specs=[
            pl.BlockSpec((tm, ATTN_W), lambda i: (i, 0)),
            pl.BlockSpec((tm, SSM_W), lambda i: (i % nt, i // nt)),
            pl.BlockSpec((tm, 2 * D_MODEL), lambda i: (i, 0)),
            pl.BlockSpec((tm, D_MODEL), lambda i: (i, 0)),
            _const_spec(wao.shape),
            _const_spec(wglu.shape),
            _const_spec(wo.shape),
            _const_spec((1, D_MODEL)),
            _const_spec(wup.shape),
            _const_spec(cw.shape),
            _const_spec(cb.shape),
            _const_spec(wdn.shape),
            _const_spec((1, D_MODEL)),
            _const_spec(tail_in.shape),
        ],
        out_specs=[pl.BlockSpec((tm, D_MODEL), lambda i: (i, 0)),
                   pl.BlockSpec(tail_in.shape, lambda i: (0, 0))],
        out_shape=[jax.ShapeDtypeStruct((bsz * seq, D_MODEL), F32),
                   jax.ShapeDtypeStruct(tail_in.shape, F32)],
        scratch_shapes=[pltpu.VMEM(tail_in.shape, F32), pltpu.VMEM((tm, D_FF), BF16),
                        pltpu.VMEM((tm, D_MODEL), F32)],
        compiler_params=_params(1),
        name="mixffn",
    )(attn, z_tm, gate, x2d, wao, wglu, wo, g2, wup, cw, cb, wdn, g3, tail_in)


def _forward(x, meta_tokens, norm_mix_g, w_in, b_forget, w_attn_out, lam_re, lam_im, b_re, b_im,
             c_re, c_im, d_skip, log_dt, w_glu, w_o, norm_ffn_g, w_up, conv_w, conv_b, w_down,
             norm_final_g, *, tm, tq, tc):
    bsz, seq, _ = x.shape
    assert seq % tm == 0 and seq % tq == 0
    assert seq % (tc * S5_CHUNKS_PER_STEP) == 0
    th = META_BLK
    assert th >= tc >= N_META
    x2d = x.reshape(bsz * seq, D_MODEL)
    head = jnp.concatenate([jnp.zeros((th - N_META, D_MODEL), F32), meta_tokens.astype(F32)], axis=0)

    q_end, k_end, v_end = ATTN_W, 2 * ATTN_W, 3 * ATTN_W
    f_end = v_end + HEADS
    u_end = f_end + SSM_W
    wq = w_in[:, :q_end].astype(BF16)
    wk = w_in[:, q_end:k_end].astype(BF16)
    wvt = jnp.transpose(w_in[:, k_end:v_end]).astype(BF16)
    place, ones = _forget_lane_weights()
    wf32 = w_in[:, v_end:f_end]
    wf_hi = wf32.astype(BF16)
    wf_lo = (wf32 - wf_hi.astype(F32)).astype(BF16)
    wf = jnp.pad(jnp.concatenate([wf_hi, wf_lo], axis=1), ((0, 0), (0, LANES - 2 * HEADS)))
    bfp = jnp.pad(b_forget.astype(F32), (0, LANES - HEADS)).reshape(1, LANES)
    wu = w_in[:, f_end:u_end].astype(BF16)
    wg = w_in[:, u_end:].astype(BF16)
    tri = lambda n: jnp.asarray(np.tril(np.ones((n, n), np.float32)), BF16)
    g1 = norm_mix_g.reshape(1, D_MODEL)
    inproj = functools.partial(_inproj, g1=g1, wq=wq, wk=wk, wvt=wvt, wu=wu, wg=wg, wf=wf,
                               bfp=bfp, place=place, ones=ones)
    q_h, k_h, vt_h, u_h, gate_h, f_meta = inproj(head, jnp.zeros((SUBLANES, LANES), F32), tri=tri(th),
                                                 bsz=1, seq=th, tm=th)
    q_m, k_m, vt_m, u_m, gate_m, _ = inproj(x2d, f_meta, tri=tri(tm), bsz=bsz, seq=seq, tm=tm)

    attn_m, attn_h = _attention(q_m, k_m, vt_m, q_h, k_h, vt_h, bsz=bsz, seq=seq, tm=th, tq=tq)

    lbr, lbi, bbr, bbi = _s5_prep(lam_re, lam_im, log_dt, b_re, b_im)
    bblk, cblk = _s5_block_weights(bbr.reshape(SSM_GROUPS, SSM_STATE, SSM_GROUP),
                                   bbi.reshape(SSM_GROUPS, SSM_STATE, SSM_GROUP), c_re, c_im)
    lamr = lbr.reshape(1, SSM_GROUPS * SSM_STATE)
    lami = lbi.reshape(1, SSM_GROUPS * SSM_STATE)
    dsk = d_skip.astype(F32).reshape(1, SSM_W)
    u_meta = jnp.pad(u_h[th - tc:], ((0, 0), (0, (S5_META_BATCH - 1) * SSM_W)))
    z_meta, st_meta = _s5(u_meta, bblk, cblk, lamr, lami, dsk,
                          jnp.zeros((S5_META_BATCH, STATE_W), F32), nb=S5_META_BATCH, tc=tc, nch=1)
    z_h = jnp.concatenate([jnp.zeros((th - tc, SSM_W), BF16), z_meta[:, :SSM_W]], axis=0)
    z_m, _ = _s5(u_m, bblk, cblk, lamr, lami, dsk, jnp.broadcast_to(st_meta[0:1], (bsz, STATE_W)),
                 nb=bsz, tc=tc, nch=S5_CHUNKS_PER_STEP)

    cw = jnp.pad(conv_w.astype(F32), ((0, SUBLANES - CONV_W), (0, 0)))
    mixffn = functools.partial(
        _mixffn, wao=w_attn_out.astype(BF16), wglu=w_glu.astype(BF16), wo=w_o.astype(BF16),
        g2=norm_ffn_g.reshape(1, D_MODEL), wup=w_up.astype(BF16), cw=cw,
        cb=conv_b.astype(F32).reshape(1, 2 * D_FF), wdn=w_down.astype(BF16),
        g3=norm_final_g.reshape(1, D_MODEL))
    _, tail = mixffn(attn_h, z_h, gate_h, head, jnp.zeros((SUBLANES, 2 * D_FF), F32),
                     bsz=1, seq=th, tm=th, zero_rows=th - N_META)
    out, _ = mixffn(attn_m, z_m, gate_m, x2d, tail, bsz=bsz, seq=seq, tm=tm)
    return out.reshape(bsz, seq, D_MODEL)


def kernel(x, meta_tokens, norm_mix_g, w_in, b_forget, w_attn_out, ssm_lambda_re, ssm_lambda_im,
           ssm_b_re, ssm_b_im, ssm_c_re, ssm_c_im, ssm_d, ssm_log_dt, w_glu, w_o, norm_ffn_g,
           w_ffn_up, ffn_conv_w, ffn_conv_b, w_ffn_down, norm_final_g):
    assert norm_mix_g.shape[0] == 1, "single-layer block"
    return _forward(x, meta_tokens, norm_mix_g[0], w_in[0], b_forget[0], w_attn_out[0],
                    ssm_lambda_re[0], ssm_lambda_im[0], ssm_b_re[0], ssm_b_im[0], ssm_c_re[0],
                    ssm_c_im[0], ssm_d[0], ssm_log_dt[0], w_glu[0], w_o[0], norm_ffn_g[0],
                    w_ffn_up[0], ffn_conv_w[0], ffn_conv_b[0], w_ffn_down[0], norm_final_g,
                    tm=TM, tq=TQ, tc=S5_CHUNK)
```
